```python
import math
import jax
import jax.numpy as jnp
from jax import lax
import numpy as np

D_MODEL = 1024
BATCH = 2
SEQ = 8192
DEPTH = 4
DEC_BATCH = 128
DEC_SEQ = 4
PAST_LEN = 2048
PAGE_SIZE = 128

N_A_LAYERS = DEPTH // 2
N_B_LAYERS = DEPTH - N_A_LAYERS
D_CONV = D_MODEL
CONV_WIDTH = 3
N_HEADS = 8
HEAD_DIM = D_MODEL // N_HEADS
Q_WIDTH = N_HEADS * HEAD_DIM
KV_WIDTH = N_HEADS * HEAD_DIM
MOBA_BLOCK = 256
MOBA_TOPK = 3
Q_CHUNK = 64
N_MEM = 256
MEM_HEADS = 4
MEM_HEAD_DIM = D_MODEL // MEM_HEADS
MEM_WIDTH = MEM_HEADS * MEM_HEAD_DIM
D_FF = 4 * D_MODEL
ALIBI_MAX_EXP = 8.0
EPS = 1e-5

kernel_name = "yoco_shortconv_moba_memxattn_step"


def rms_norm(x, g):
    xf = x.astype(jnp.float32)
    y = xf * lax.rsqrt(jnp.mean(xf * xf, axis=-1, keepdims=True) + EPS)
    return (y * g.astype(jnp.float32)).astype(x.dtype)


def alibi_slopes(n_heads):
    return jnp.exp2(-ALIBI_MAX_EXP * jnp.arange(1, n_heads + 1, dtype=jnp.float32) / n_heads)


def sqrelu_mlp(h, w_up, w_down):
    return jnp.square(jax.nn.relu(h @ w_up)) @ w_down


def short_conv(u, buf, w):
    t_len = u.shape[1]
    up = jnp.concatenate([buf.astype(u.dtype), u], axis=1)
    z = w[0] * up[:, 0:t_len]
    for j in range(1, CONV_WIDTH):
        z = z + w[j] * up[:, j:j + t_len]
    return z, up[:, -(CONV_WIDTH - 1):]


def memory_kv(mem, g_mem, w_mem_kv):
    b, m, _ = mem.shape
    hn = rms_norm(mem[None], g_mem[:, None, None, :])
    kv = jnp.einsum('lbmd,lde->lbme', hn, w_mem_kv)
    k = kv[..., :MEM_WIDTH].reshape(DEPTH, b, m, MEM_HEADS, MEM_HEAD_DIM)
    v = kv[..., MEM_WIDTH:].reshape(DEPTH, b, m, MEM_HEADS, MEM_HEAD_DIM)
    return k, v


def memory_attend(q, mem_k, mem_v):
    b, t_len = q.shape[:2]
    logits = jnp.einsum('bthd,bmhd->bhtm', q, mem_k, preferred_element_type=jnp.float32) * (MEM_HEAD_DIM ** -0.5)
    p = jax.nn.softmax(logits, axis=-1).astype(mem_v.dtype)
    o = jnp.einsum('bhtm,bmhd->bthd', p, mem_v)
    return o.reshape(b, t_len, MEM_WIDTH)


def gather_pages(pool, page_table):
    pages = pool[page_table]
    return pages.reshape(page_table.shape[0], -1, pool.shape[2], pool.shape[3])


def prepare_shared_kv(past_k, past_v, k_new, v_new):
    b, _, h, dh = k_new.shape
    length = past_k.shape[1] + k_new.shape[1]
    n_blk = -(-length // MOBA_BLOCK) + 1
    pad = jnp.zeros((b, n_blk * MOBA_BLOCK - length, h, dh), k_new.dtype)
    kp = jnp.concatenate([past_k.astype(k_new.dtype), k_new, pad], axis=1)
    vp = jnp.concatenate([past_v.astype(v_new.dtype), v_new, pad], axis=1)
    k_mean = jnp.mean(kp.reshape(b, n_blk, MOBA_BLOCK, h, dh).astype(jnp.float32), axis=2)
    return kp, vp, k_mean, length


def moba_attend(q, kp, vp, k_mean, length):
    b, t_len, h, dh = q.shape
    q_offset = length - t_len
    n_blk = kp.shape[1] // MOBA_BLOCK
    nb = n_blk - 1
    qc = math.gcd(t_len, Q_CHUNK)
    n_chunks = t_len // qc
    win = MOBA_BLOCK + qc
    n_sel = min(MOBA_TOPK, nb)
    slopes = alibi_slopes(h)
    scale = dh ** -0.5
    blk_ids = jnp.arange(n_blk)
    h_ids = jnp.arange(h)[None, :, None]
    j_ids = jnp.arange(MOBA_BLOCK)
    w_ids = jnp.arange(win)

    def chunk(args):
        q_c, bi, start = args
        t = start + jnp.arange(qc)
        own = t // MOBA_BLOCK
        k_seq = lax.dynamic_index_in_dim(kp, bi, 0, keepdims=False)
        v_seq = lax.dynamic_index_in_dim(vp, bi, 0, keepdims=False)
        mean_b = lax.dynamic_index_in_dim(k_mean, bi, 0, keepdims=False)
        gate = jnp.einsum('qhd,nhd->qhn', q_c.astype(jnp.float32), mean_b)
        gate = jnp.where((blk_ids[None, :] < own[:, None])[:, None, :], gate, -jnp.inf)
        g_top, sel = lax.top_k(gate, n_sel)
        sel_ok = g_top > -jnp.inf
        k_blk = k_seq.reshape(n_blk, MOBA_BLOCK, h, dh).transpose(2, 0, 1, 3)
        v_blk = v_seq.reshape(n_blk, MOBA_BLOCK, h, dh).transpose(2, 0, 1, 3)
        k_g = k_blk[h_ids, sel]
        v_g = v_blk[h_ids, sel]
        s_pos = sel[..., None] * MOBA_BLOCK + j_ids
        l_sel = (jnp.einsum('qhd,qhnjd->qhnj', q_c, k_g, preferred_element_type=jnp.float32) * scale
                 - slopes[None, :, None, None] * (t[:, None, None, None] - s_pos).astype(jnp.float32))
        l_sel = jnp.where(sel_ok[..., None], l_sel, -jnp.inf).reshape(qc, h, n_sel * MOBA_BLOCK)
        ws = (start // MOBA_BLOCK) * MOBA_BLOCK
        k_w = lax.dynamic_slice_in_dim(k_seq, ws, win, axis=0)
        v_w = lax.dynamic_slice_in_dim(v_seq, ws, win, axis=0)
        w_pos = ws + w_ids
        ok_w = (w_pos[None, :] <= t[:, None]) & ((w_pos // MOBA_BLOCK)[None, :] == own[:, None])
        l_w = (jnp.einsum('qhd,whd->qhw', q_c, k_w, preferred_element_type=jnp.float32) * scale
               - slopes[None, :, None] * (t[:, None] - w_pos[None, :]).astype(jnp.float32)[:, None, :])
        l_w = jnp.where(ok_w[:, None, :], l_w, -jnp.inf)
        p = jax.nn.softmax(jnp.concatenate([l_sel, l_w], axis=-1), axis=-1).astype(v_w.dtype)
        p_sel = p[..., :n_sel * MOBA_BLOCK].reshape(qc, h, n_sel, MOBA_BLOCK)
        p_w = p[..., n_sel * MOBA_BLOCK:]
        return jnp.einsum('qhnj,qhnjd->qhd', p_sel, v_g) + jnp.einsum('qhw,whd->qhd', p_w, v_w)

    q_chunks = q.reshape(b * n_chunks, qc, h, dh)
    b_idx = jnp.repeat(jnp.arange(b, dtype=jnp.int32), n_chunks)
    starts = q_offset + jnp.tile(jnp.arange(n_chunks, dtype=jnp.int32) * qc, b)
    out = lax.map(chunk, (q_chunks, b_idx, starts))
    return out.reshape(b, t_len, h * dh)


def trunk(x, conv_state, mem_k, mem_v, past_k, past_v,
          g_mix, w_in_a, conv_w, w_out_a, g_kv, w_kv, w_in_b, w_out_b,
          g_mlp, w_up, w_down, g_final):
    b, t_len = x.shape[:2]
    new_conv = []
    k_new = v_new = None
    shared = None
    for layer in range(DEPTH):
        h = rms_norm(x, g_mix[layer])
        if layer < N_A_LAYERS:
            a = layer
            proj = h @ w_in_a[a]
            bg, cg, hv, qm = jnp.split(proj, [D_CONV, 2 * D_CONV, 3 * D_CONV], axis=-1)
            z, buf = short_conv(cg * hv, conv_state[a], conv_w[a])
            new_conv.append(buf)
            tok = bg * z
            w_out = w_out_a[a]
        else:
            bi = layer - N_A_LAYERS
            proj = h @ w_in_b[bi]
            q, qm = jnp.split(proj, [Q_WIDTH], axis=-1)
            kp, vp, k_mean, length = shared
            tok = moba_attend(q.reshape(b, t_len, N_HEADS, HEAD_DIM), kp, vp, k_mean, length)
            w_out = w_out_b[bi]
        mo = memory_attend(qm.reshape(b, t_len, MEM_HEADS, MEM_HEAD_DIM), mem_k[layer], mem_v[layer])
        x = x + jnp.concatenate([tok, mo], axis=-1) @ w_out
        x = x + sqrelu_mlp(rms_norm(x, g_mlp[layer]), w_up[layer], w_down[layer])
        if layer == N_A_LAYERS - 1:
            kv = rms_norm(x, g_kv) @ w_kv
            k_new = kv[..., :KV_WIDTH].reshape(b, t_len, N_HEADS, HEAD_DIM)
            v_new = kv[..., KV_WIDTH:].reshape(b, t_len, N_HEADS, HEAD_DIM)
            shared = prepare_shared_kv(past_k, past_v, k_new, v_new)
    y = rms_norm(x, g_final)
    return y, jnp.stack(new_conv), k_new, v_new


def setup_inputs(seed: int = 0) -> dict:
    key = jax.random.key(seed)
    ks = jax.random.split(key, 32)
    f32 = jnp.float32

    def nrm(k, shape, scale):
        return jax.random.normal(k, shape, f32) * scale

    n_pages = PAST_LEN // PAGE_SIZE
    n_used = DEC_BATCH * n_pages
    n_phys = n_used + n_used // 4
    page_table = jax.random.permutation(ks[0], n_phys)[:n_used].reshape(DEC_BATCH, n_pages).astype(jnp.int32)
    return {
        "x_prompt": nrm(ks[1], (BATCH, SEQ, D_MODEL), 1.0),
        "x_sample": nrm(ks[2], (DEC_BATCH, DEC_SEQ, D_MODEL), 1.0),
        "state_conv": nrm(ks[3], (N_A_LAYERS, DEC_BATCH, CONV_WIDTH - 1, D_CONV), 1.0),
        "cache_k": nrm(ks[4], (n_phys, PAGE_SIZE, N_HEADS, HEAD_DIM), 1.0),
        "cache_v": nrm(ks[5], (n_phys, PAGE_SIZE, N_HEADS, HEAD_DIM), 1.0),
        "cache_mem_k": nrm(ks[6], (DEPTH, DEC_BATCH, N_MEM, MEM_HEADS, MEM_HEAD_DIM), 1.0),
        "cache_mem_v": nrm(ks[7], (DEPTH, DEC_BATCH, N_MEM, MEM_HEADS, MEM_HEAD_DIM), 1.0),
        "page_table": page_table,
        "mem_prompt": nrm(ks[8], (BATCH, N_MEM, D_MODEL), 1.0),
        "g_mix": 1.0 + nrm(ks[9], (DEPTH, D_MODEL), 0.05),
        "w_in_a": nrm(ks[10], (N_A_LAYERS, D_MODEL, 3 * D_CONV + MEM_WIDTH), D_MODEL ** -0.5),
        "conv_w": nrm(ks[11], (N_A_LAYERS, CONV_WIDTH, D_CONV), CONV_WIDTH ** -0.5),
        "w_out_a": nrm(ks[12], (N_A_LAYERS, D_CONV + MEM_WIDTH, D_MODEL), (D_CONV + MEM_WIDTH) ** -0.5),
        "g_kv": 1.0 + nrm(ks[13], (D_MODEL,), 0.05),
        "w_kv": nrm(ks[14], (D_MODEL, 2 * KV_WIDTH), D_MODEL ** -0.5),
        "w_in_b": nrm(ks[15], (N_B_LAYERS, D_MODEL, Q_WIDTH + MEM_WIDTH), D_MODEL ** -0.5),
        "w_out_b": nrm(ks[16], (N_B_LAYERS, Q_WIDTH + MEM_WIDTH, D_MODEL), (Q_WIDTH + MEM_WIDTH) ** -0.5),
        "g_mem": 1.0 + nrm(ks[17], (DEPTH, D_MODEL), 0.05),
        "w_mem_kv": nrm(ks[18], (DEPTH, D_MODEL, 2 * MEM_WIDTH), D_MODEL ** -0.5),
        "g_mlp": 1.0 + nrm(ks[19], (DEPTH, D_MODEL), 0.05),
        "w_up": nrm(ks[20], (DEPTH, D_MODEL, D_FF), D_MODEL ** -0.5),
        "w_down": nrm(ks[21], (DEPTH, D_FF, D_MODEL), D_FF ** -0.5),
        "g_final": 1.0 + nrm(ks[22], (D_MODEL,), 0.05),
    }


def reference(x_prompt, x_sample, state_conv, cache_k, cache_v, cache_mem_k, cache_mem_v, page_table,
              mem_prompt, g_mix, w_in_a, conv_w, w_out_a, g_kv, w_kv, w_in_b, w_out_b,
              g_mem, w_mem_kv, g_mlp, w_up, w_down, g_final):
    b = x_prompt.shape[0]
    mem_k_p, mem_v_p = memory_kv(mem_prompt, g_mem, w_mem_kv)
    zero_conv = jnp.zeros((N_A_LAYERS, b, CONV_WIDTH - 1, D_CONV), x_prompt.dtype)
    empty_kv = jnp.zeros((b, 0, N_HEADS, HEAD_DIM), x_prompt.dtype)
    y_prompt, conv_p, k_p, v_p = trunk(
        x_prompt, zero_conv, mem_k_p, mem_v_p, empty_kv, empty_kv,
        g_mix, w_in_a, conv_w, w_out_a, g_kv, w_kv, w_in_b, w_out_b, g_mlp, w_up, w_down, g_final)
    past_k = gather_pages(cache_k, page_table)
    past_v = gather_pages(cache_v, page_table)
    y_sample, conv_s, k_s, v_s = trunk(
        x_sample, state_conv, cache_mem_k, cache_mem_v, past_k, past_v,
        g_mix, w_in_a, conv_w, w_out_a, g_kv, w_kv, w_in_b, w_out_b, g_mlp, w_up, w_down, g_final)
    return (y_prompt, y_sample, conv_p, conv_s, k_p, v_p, k_s, v_s, mem_k_p, mem_v_p)
```

```python
import functools

import jax
import jax.numpy as jnp
from jax import lax
from jax.experimental import pallas as pl
from jax.experimental.pallas import tpu as pltpu

F32 = jnp.float32
BF16 = jnp.bfloat16

D_MODEL = 1024
DEPTH = 4
N_A_LAYERS = DEPTH // 2
CONV_WIDTH = 3
N_HEADS = 8
HEAD_DIM = D_MODEL // N_HEADS
MOBA_BLOCK = 256
MOBA_TOPK = 3
N_MEM = 256
MEM_HEADS = 4
MEM_HEAD_DIM = D_MODEL // MEM_HEADS
D_FF = 4 * D_MODEL
ALIBI_MAX_EXP = 8.0
EPS = 1e-5
PAGE_SIZE = 128

LANES = 128
SUBLANES = 8
VMEM_LIMIT_BYTES = 56 * 1024 * 1024

TOKEN_TILE = 512
AUG_WIDTH = 2 * HEAD_DIM
N_BLOCK_COLS = 40
MASK_NEG = -(2.0 ** 100)
MEM_BATCH_TILE = 4


def _compiler_params(n_grid_dims):
    return pltpu.CompilerParams(
        dimension_semantics=("arbitrary",) * n_grid_dims,
        vmem_limit_bytes=VMEM_LIMIT_BYTES)


def _resident(shape):
    zeros = (0,) * len(shape)
    return pl.BlockSpec(shape, lambda *_: zeros, pipeline_mode=pl.Buffered(1))


def _rms(x, g):
    return x * lax.rsqrt(jnp.mean(x * x, axis=-1, keepdims=True) + EPS) * g


def _dot(a, b):
    return jnp.dot(a, b, preferred_element_type=F32)


def _dot_nt(a, b):
    return lax.dot_general(a, b, (((1,), (1,)), ((), ())), preferred_element_type=F32)


def _softmax_pv(s, v):
    m = jnp.max(s, axis=-1, keepdims=True)
    p = jnp.exp(s - m)
    l = jnp.sum(p, axis=-1, keepdims=True)
    return _dot(p.astype(BF16), v) / l


def _mem_kv_kernel(mem_ref, g_ref, w_ref, k_ref, v_ref, kb_ref, vb_ref):
    hn = _rms(mem_ref[...], g_ref[...]).astype(BF16)
    kv = _dot(hn, w_ref[...])
    k = kv[:, :D_MODEL]
    v = kv[:, D_MODEL:]
    k_ref[...] = k
    v_ref[...] = v
    kb_ref[...] = k.astype(BF16)
    vb_ref[...] = v.astype(BF16)


def _mem_kv(mem2d, g_mem, w_mem_kv_b):
    rows = mem2d.shape[0]
    out_f = jax.ShapeDtypeStruct((DEPTH, rows, D_MODEL), F32)
    out_b = jax.ShapeDtypeStruct((DEPTH, rows, D_MODEL), BF16)
    blk = pl.BlockSpec((None, rows, D_MODEL), lambda l: (l, 0, 0))
    return pl.pallas_call(
        _mem_kv_kernel,
        grid=(DEPTH,),
        in_specs=[
            pl.BlockSpec((rows, D_MODEL), lambda l: (0, 0)),
            pl.BlockSpec((None, 1, D_MODEL), lambda l: (l, 0, 0)),
            pl.BlockSpec((None, D_MODEL, 2 * D_MODEL), lambda l: (l, 0, 0)),
        ],
        out_specs=[blk, blk, blk, blk],
        out_shape=[out_f, out_f, out_b, out_b],
        compiler_params=_compiler_params(1),
        name="mem_kv",
    )(mem2d, g_mem.reshape(DEPTH, 1, D_MODEL), w_mem_kv_b)


def _conv_gate(h, w_ref, cw_ref, fix_prev):
    cg = _dot(h, w_ref[:, D_MODEL:2 * D_MODEL])
    hv = _dot(h, w_ref[:, 2 * D_MODEL:3 * D_MODEL])
    u = cg * hv
    prev1, prev2 = fix_prev(pltpu.roll(u, 1, 0), pltpu.roll(u, 2, 0))
    cw = cw_ref[...]
    z = cw[0:1, :] * prev2 + cw[1:2, :] * prev1 + cw[2:3, :] * u
    bg = _dot(h, w_ref[:, 0:D_MODEL])
    qm = _dot(h, w_ref[:, 3 * D_MODEL:4 * D_MODEL])
    return bg * z, qm, u


def _a_front_prompt_kernel(x_ref, g_ref, w_ref, cw_ref, tok_ref, qm_ref, ulast_ref, carry_ref):
    @pl.when(pl.program_id(1) == 0)
    def _():
        carry_ref[...] = jnp.zeros_like(carry_ref)

    h = _rms(x_ref[...], g_ref[...]).astype(BF16)
    carry = carry_ref[...]

    def fix_prev(prev1, prev2):
        row = lax.broadcasted_iota(jnp.int32, prev1.shape, 0)
        c_m1 = carry[SUBLANES - 1:SUBLANES, :]
        c_m2 = carry[SUBLANES - 2:SUBLANES - 1, :]
        prev1 = jnp.where(row < 1, c_m1, prev1)
        prev2 = jnp.where(row < 1, c_m2, jnp.where(row < 2, c_m1, prev2))
        return prev1, prev2

    tok, qm, u = _conv_gate(h, w_ref, cw_ref, fix_prev)
    tok_ref[...] = tok.astype(BF16)
    qm_ref[...] = qm.astype(BF16)
    tail = u[TOKEN_TILE - SUBLANES:, :]
    carry_ref[...] = tail
    ulast_ref[...] = tail


def _a_front_prompt(x2d, batch, g, w_b, conv_w):
    rows = x2d.shape[0]
    tiles = rows // batch // TOKEN_TILE
    tile_spec = pl.BlockSpec((TOKEN_TILE, D_MODEL), lambda b, i: (b * tiles + i, 0))
    return pl.pallas_call(
        _a_front_prompt_kernel,
        grid=(batch, tiles),
        in_specs=[
            tile_spec,
            _resident((1, D_MODEL)),
            _resident((D_MODEL, 4 * D_MODEL)),
            _resident((CONV_WIDTH, D_MODEL)),
        ],
        out_specs=[
            tile_spec,
            tile_spec,
            pl.BlockSpec((SUBLANES, D_MODEL), lambda b, i: (b * tiles + i, 0)),
        ],
        out_shape=[
            jax.ShapeDtypeStruct((rows, D_MODEL), BF16),
            jax.ShapeDtypeStruct((rows, D_MODEL), BF16),
            jax.ShapeDtypeStruct((batch * tiles * SUBLANES, D_MODEL), F32),
        ],
        scratch_shapes=[pltpu.VMEM((SUBLANES, D_MODEL), F32)],
        compiler_params=_compiler_params(2),
        name="a_front_prompt",
    )(x2d, g.reshape(1, D_MODEL), w_b, conv_w)


def _a_front_sample_kernel(seq, x_ref, g_ref, w_ref, cw_ref, s1_ref, s2_ref, tok_ref, qm_ref, u_ref):
    h = _rms(x_ref[...], g_ref[...]).astype(BF16)

    def fix_prev(prev1, prev2):
        pos = lax.broadcasted_iota(jnp.int32, prev1.shape, 0) % seq
        return (jnp.where(pos >= 1, prev1, s1_ref[...]),
                jnp.where(pos >= 2, prev2, s2_ref[...]))

    tok, qm, u = _conv_gate(h, w_ref, cw_ref, fix_prev)
    tok_ref[...] = tok.astype(BF16)
    qm_ref[...] = qm
    u_ref[...] = u


def _a_front_sample(x2d, seq, g, w_b, conv_w, state):
    rows = x2d.shape[0]
    n_seq = rows // seq
    assert seq >= CONV_WIDTH - 1
    zeros = jnp.zeros((n_seq, seq - 1, D_MODEL), F32)
    s1 = jnp.concatenate([state[:, 1:2], zeros], axis=1).reshape(rows, D_MODEL)
    s2 = jnp.concatenate([state, zeros[:, 1:]], axis=1).reshape(rows, D_MODEL)
    full = pl.BlockSpec((rows, D_MODEL), lambda i: (0, 0))
    return pl.pallas_call(
        functools.partial(_a_front_sample_kernel, seq),
        grid=(1,),
        in_specs=[full, _resident((1, D_MODEL)), _resident((D_MODEL, 4 * D_MODEL)),
                  _resident((CONV_WIDTH, D_MODEL)), full, full],
        out_specs=[full, full, full],
        out_shape=[
            jax.ShapeDtypeStruct((rows, D_MODEL), BF16),
            jax.ShapeDtypeStruct((rows, D_MODEL), F32),
            jax.ShapeDtypeStruct((rows, D_MODEL), F32),
        ],
        compiler_params=_compiler_params(1),
        name="a_front_sample",
    )(x2d, g.reshape(1, D_MODEL), w_b, conv_w, s1, s2)


def _mem_attn_prompt_kernel(q_ref, k_ref, v_ref, o_ref):
    scale = MEM_HEAD_DIM ** -0.5
    for hd in range(MEM_HEADS):
        sl = slice(hd * MEM_HEAD_DIM, (hd + 1) * MEM_HEAD_DIM)
        s = _dot_nt(q_ref[:, sl], k_ref[:, sl]) * scale
        o_ref[:, sl] = _softmax_pv(s, v_ref[:, sl]).astype(o_ref.dtype)


def _mem_attn_prompt(qm, batch, k_b, v_b):
    rows = qm.shape[0]
    tiles = rows // batch // TOKEN_TILE
    tile_spec = pl.BlockSpec((TOKEN_TILE, D_MODEL), lambda b, i: (b * tiles + i, 0))
    kv_spec = pl.BlockSpec((N_MEM, D_MODEL), lambda b, i: (b, 0))
    return pl.pallas_call(
        _mem_attn_prompt_kernel,
        grid=(batch, tiles),
        in_specs=[tile_spec, kv_spec, kv_spec],
        out_specs=tile_spec,
        out_shape=jax.ShapeDtypeStruct((rows, D_MODEL), BF16),
        compiler_params=_compiler_params(2),
        name="mem_attn_prompt",
    )(qm, k_b, v_b)


def _mem_attn_sample_kernel(seq, q_ref, k_ref, v_ref, o_ref):
    scale = MEM_HEAD_DIM ** -0.5
    pad = jnp.zeros((SUBLANES - seq, MEM_HEAD_DIM), F32)
    for bb in range(MEM_BATCH_TILE):
        for hd in range(MEM_HEADS):
            sl = slice(hd * MEM_HEAD_DIM, (hd + 1) * MEM_HEAD_DIM)
            q = jnp.concatenate([q_ref[bb, :, sl], pad], axis=0).astype(BF16)
            s = _dot_nt(q, k_ref[bb, :, sl].astype(BF16)) * scale
            o = _softmax_pv(s, v_ref[bb, :, sl].astype(BF16))
            o_ref[bb, :, sl] = o[:seq, :].astype(o_ref.dtype)


def _mem_attn_sample(qm3, layer, cache_k4, cache_v4):
    n_seq, seq, _ = qm3.shape
    q_spec = pl.BlockSpec((MEM_BATCH_TILE, seq, D_MODEL), lambda i: (i, 0, 0))
    kv_spec = pl.BlockSpec((None, MEM_BATCH_TILE, N_MEM, D_MODEL), lambda i: (layer, i, 0, 0))
    return pl.pallas_call(
        functools.partial(_mem_attn_sample_kernel, seq),
        grid=(n_seq // MEM_BATCH_TILE,),
        in_specs=[q_spec, kv_spec, kv_spec],
        out_specs=q_spec,
        out_shape=jax.ShapeDtypeStruct((n_seq, seq, D_MODEL), F32),
        compiler_params=_compiler_params(1),
        name="mem_attn_sample",
    )(qm3, cache_k4, cache_v4)


def _post_kernel(final_norm, x_ref, tok_ref, mo_ref, wo_ref, g_ref, wu_ref, wd_ref, gf_ref, o_ref):
    o_ref[...] = (x_ref[...]
                  + _dot(tok_ref[...].astype(BF16), wo_ref[0:D_MODEL, :])
                  + _dot(mo_ref[...].astype(BF16), wo_ref[D_MODEL:2 * D_MODEL, :]))
    x = o_ref[...]
    h = _rms(x, g_ref[...]).astype(BF16)
    for c in range(D_FF // D_MODEL):
        sl = slice(c * D_MODEL, (c + 1) * D_MODEL)
        a = jnp.square(jnp.maximum(_dot(h, wu_ref[:, sl]), 0.0)).astype(BF16)
        x = x + _dot(a, wd_ref[sl, :])
    o_ref[...] = x
    if final_norm:
        o_ref[...] = _rms(o_ref[...], gf_ref[...])


def _post(x2d, tok, mo, wo_b, g, wu_b, wd_b, g_final, final_norm):
    rows = x2d.shape[0]
    tile_spec = pl.BlockSpec((TOKEN_TILE, D_MODEL), lambda i: (i, 0))
    return pl.pallas_call(
        functools.partial(_post_kernel, final_norm),
        grid=(rows // TOKEN_TILE,),
        in_specs=[
            tile_spec, tile_spec, tile_spec,
            _resident((2 * D_MODEL, D_MODEL)),
            _resident((1, D_MODEL)),
            _resident((D_MODEL, D_FF)),
            _resident((D_FF, D_MODEL)),
            _resident((1, D_MODEL)),
        ],
        out_specs=tile_spec,
        out_shape=jax.ShapeDtypeStruct((rows, D_MODEL), F32),
        compiler_params=_compiler_params(1),
        name="post_final" if final_norm else "post",
    )(x2d, tok, mo, wo_b, g.reshape(1, D_MODEL), wu_b, wd_b, g_final.reshape(1, D_MODEL))


def _kv_prompt_kernel(x_ref, g_ref, w_ref, kaux_ref, k_ref, v_ref, kaug_ref, vb_ref, kmean_ref):
    h = _rms(x_ref[...], g_ref[...]).astype(BF16)
    kv = _dot(h, w_ref[...])
    k = kv[:, :D_MODEL]
    v = kv[:, D_MODEL:]
    k_ref[...] = k
    v_ref[...] = v
    vb_ref[...] = v.astype(BF16)
    kmean_ref[...] = jnp.sum(k, axis=0, keepdims=True) * (1.0 / MOBA_BLOCK)
    for hd in range(N_HEADS):
        src = slice(hd * HEAD_DIM, (hd + 1) * HEAD_DIM)
        kaug_ref[:, hd * AUG_WIDTH:hd * AUG_WIDTH + HEAD_DIM] = k[:, src].astype(BF16)
        kaug_ref[:, hd * AUG_WIDTH + HEAD_DIM:(hd + 1) * AUG_WIDTH] = kaux_ref[:, src]


def _kv_prompt(x2d, g, w_b, kaux):
    rows = x2d.shape[0]
    n_blocks = rows // MOBA_BLOCK
    seq_blocks = kaux.shape[0] // MOBA_BLOCK
    tile = lambda width: pl.BlockSpec((MOBA_BLOCK, width), lambda i: (i, 0))
    return pl.pallas_call(
        _kv_prompt_kernel,
        grid=(n_blocks,),
        in_specs=[
            tile(D_MODEL),
            _resident((1, D_MODEL)),
            _resident((D_MODEL, 2 * D_MODEL)),
            pl.BlockSpec((MOBA_BLOCK, D_MODEL), lambda i: (i % seq_blocks, 0)),
        ],
        out_specs=[
            tile(D_MODEL), tile(D_MODEL), tile(N_HEADS * AUG_WIDTH), tile(D_MODEL),
            pl.BlockSpec((None, 1, D_MODEL), lambda i: (i, 0, 0)),
        ],
        out_shape=[
            jax.ShapeDtypeStruct((rows, D_MODEL), F32),
            jax.ShapeDtypeStruct((rows, D_MODEL), F32),
            jax.ShapeDtypeStruct((rows, N_HEADS * AUG_WIDTH), BF16),
            jax.ShapeDtypeStruct((rows, D_MODEL), BF16),
            jax.ShapeDtypeStruct((n_blocks, 1, D_MODEL), F32),
        ],
        compiler_params=_compiler_params(1),
        name="kv_prompt",
    )(x2d, g.reshape(1, D_MODEL), w_b, kaux)


def _kv_sample_kernel(x_ref, g_ref, w_ref, k_ref, v_ref):
    h = _rms(x_ref[...], g_ref[...]).astype(BF16)
    kv = _dot(h, w_ref[...])
    k_ref[...] = kv[:, :D_MODEL]
    v_ref[...] = kv[:, D_MODEL:]


def _kv_sample(x2d, g, w_b):
    rows = x2d.shape[0]
    tile_spec = pl.BlockSpec((TOKEN_TILE, D_MODEL), lambda i: (i, 0))
    out = jax.ShapeDtypeStruct((rows, D_MODEL), F32)
    return pl.pallas_call(
        _kv_sample_kernel,
        grid=(rows // TOKEN_TILE,),
        in_specs=[tile_spec, _resident((1, D_MODEL)), _resident((D_MODEL, 2 * D_MODEL))],
        out_specs=[tile_spec, tile_spec],
        out_shape=[out, out],
        compiler_params=_compiler_params(1),
        name="kv_sample",
    )(x2d, g.reshape(1, D_MODEL), w_b)


def _b_front_prompt_kernel(tiles, x_ref, g_ref, w_ref, km_ref, qaux_ref, qaug_ref, qm_ref):
    h = _rms(x_ref[...], g_ref[...]).astype(BF16)
    qm_ref[...] = _dot(h, w_ref[:, D_MODEL:]).astype(BF16)
    q = _dot(h, w_ref[:, :D_MODEL])
    scale = HEAD_DIM ** -0.5
    t0 = (pl.program_id(0) % tiles) * TOKEN_TILE
    shape = (LANES, TOKEN_TILE)
    blk = lax.broadcasted_iota(jnp.int32, shape, 0)
    own = (t0 + lax.broadcasted_iota(jnp.int32, shape, 1)) // MOBA_BLOCK
    eligible = blk < own
    for hd in range(N_HEADS):
        qh = q[:, hd * HEAD_DIM:(hd + 1) * HEAD_DIM]
        q_hi = qh.astype(BF16)
        q_lo = (qh - q_hi.astype(F32)).astype(BF16)
        gate = _dot_nt(km_ref[hd], jnp.concatenate([q_hi, q_lo, q_hi], axis=1))
        gate = jnp.where(eligible, gate, -jnp.inf)
        masked = eligible
        for _ in range(MOBA_TOPK):
            best = jnp.max(gate, axis=0, keepdims=True)
            is_best = (gate == best) & (best > -jnp.inf)
            first = jnp.min(jnp.where(is_best, blk, LANES), axis=0, keepdims=True)
            pick = blk == first
            masked = masked & jnp.logical_not(pick)
            gate = jnp.where(pick, -jnp.inf, gate)
        neg = jnp.where(masked, MASK_NEG, 0.0)
        aux = neg.T + qaux_ref[:, hd * HEAD_DIM:(hd + 1) * HEAD_DIM]
        qaug_ref[:, hd * AUG_WIDTH:hd * AUG_WIDTH + HEAD_DIM] = (qh * scale).astype(BF16)
        qaug_ref[:, hd * AUG_WIDTH + HEAD_DIM:(hd + 1) * AUG_WIDTH] = aux.astype(BF16)


def _b_front_prompt(x2d, batch, g, w_b, km3, qaux):
    rows = x2d.shape[0]
    tiles = rows // batch // TOKEN_TILE
    tile = lambda width: pl.BlockSpec((TOKEN_TILE, width), lambda i: (i, 0))
    return pl.pallas_call(
        functools.partial(_b_front_prompt_kernel, tiles),
        grid=(batch * tiles,),
        in_specs=[
            tile(D_MODEL),
            _resident((1, D_MODEL)),
            _resident((D_MODEL, 2 * D_MODEL)),
            pl.BlockSpec((None, N_HEADS, LANES, 3 * HEAD_DIM), lambda i: (i // tiles, 0, 0, 0)),
            pl.BlockSpec((TOKEN_TILE, D_MODEL), lambda i: (i % tiles, 0)),
        ],
        out_specs=[tile(N_HEADS * AUG_WIDTH), tile(D_MODEL)],
        out_shape=[
            jax.ShapeDtypeStruct((rows, N_HEADS * AUG_WIDTH), BF16),
            jax.ShapeDtypeStruct((rows, D_MODEL), BF16),
        ],
        compiler_params=_compiler_params(1),
        name="b_front_prompt",
    )(x2d, g.reshape(1, D_MODEL), w_b, km3, qaux)


def _b_front_sample_kernel(x_ref, g_ref, w_ref, q_ref, qm_ref):
    h = _rms(x_ref[...], g_ref[...]).astype(BF16)
    q_ref[...] = _dot(h, w_ref[:, :D_MODEL])
    qm_ref[...] = _dot(h, w_ref[:, D_MODEL:])


def _b_front_sample(x2d, g, w_b):
    rows = x2d.shape[0]
    tile_spec = pl.BlockSpec((TOKEN_TILE, D_MODEL), lambda i: (i, 0))
    return pl.pallas_call(
        _b_front_sample_kernel,
        grid=(rows // TOKEN_TILE,),
        in_specs=[tile_spec, _resident((1, D_MODEL)), _resident((D_MODEL, 2 * D_MODEL))],
        out_specs=[tile_spec, tile_spec],
        out_shape=[jax.ShapeDtypeStruct((rows, D_MODEL), F32),
                   jax.ShapeDtypeStruct((rows, D_MODEL), F32)],
        compiler_params=_compiler_params(1),
        name="b_front_sample",
    )(x2d, g.reshape(1, D_MODEL), w_b)


def _moba_prompt_kernel(q_ref, k_ref, v_ref, o_ref):
    i = pl.program_id(2)
    q = q_ref[...]
    own = pl.multiple_of(i * MOBA_BLOCK, MOBA_BLOCK)
    s = _dot_nt(q, k_ref[pl.ds(own, MOBA_BLOCK), :])
    row = lax.broadcasted_iota(jnp.int32, s.shape, 0)
    col = lax.broadcasted_iota(jnp.int32, s.shape, 1)
    s = jnp.where(col <= row, s, MASK_NEG)
    m = jnp.max(s, axis=-1, keepdims=True)
    p = jnp.exp(s - m)
    l = jnp.sum(p, axis=-1, keepdims=True)
    acc = _dot(p.astype(BF16), v_ref[pl.ds(own, MOBA_BLOCK), :])

    def body(n, carry):
        m, l, acc = carry
        start = pl.multiple_of(n * MOBA_BLOCK, MOBA_BLOCK)
        s = _dot_nt(q, k_ref[pl.ds(start, MOBA_BLOCK), :])
        m_new = jnp.maximum(m, jnp.max(s, axis=-1, keepdims=True))
        alpha = jnp.exp(m - m_new)
        p = jnp.exp(s - m_new)
        l = alpha * l + jnp.sum(p, axis=-1, keepdims=True)
        acc = alpha * acc + _dot(p.astype(BF16), v_ref[pl.ds(start, MOBA_BLOCK), :])
        return m_new, l, acc

    m, l, acc = lax.fori_loop(0, i, body, (m, l, acc))
    o_ref[...] = (acc / l).astype(o_ref.dtype)


def _moba_prompt(qaug, kaug, vb, batch):
    rows = qaug.shape[0]
    seq = rows // batch
    q_blocks = seq // MOBA_BLOCK
    return pl.pallas_call(
        _moba_prompt_kernel,
        grid=(batch, N_HEADS, q_blocks),
        in_specs=[
            pl.BlockSpec((MOBA_BLOCK, AUG_WIDTH), lambda b, h, i: (b * q_blocks + i, h)),
            pl.BlockSpec((seq, AUG_WIDTH), lambda b, h, i: (b, h)),
            pl.BlockSpec((seq, HEAD_DIM), lambda b, h, i: (b, h)),
        ],
        out_specs=pl.BlockSpec((MOBA_BLOCK, HEAD_DIM), lambda b, h, i: (b * q_blocks + i, h)),
        out_shape=jax.ShapeDtypeStruct((rows, D_MODEL), BF16),
        compiler_params=_compiler_params(3),
        name="moba_prompt",
    )(qaug, kaug, vb)


def _moba_sample_kernel(seq, past_len, n_pages, pt_ref, q_ref, kn_ref, vn_ref, slope_ref, *refs):
    del pt_ref
    k_pages = refs[:n_pages]
    v_pages = refs[n_pages:2 * n_pages]
    o_ref = refs[2 * n_pages]
    pages_per_block = MOBA_BLOCK // PAGE_SIZE
    n_past = n_pages // pages_per_block
    rows = seq * N_HEADS
    scale = HEAD_DIM ** -0.5

    q = q_ref[...]
    q_rep = jnp.concatenate(
        [jnp.broadcast_to(q[qi:qi + 1, :], (N_HEADS, D_MODEL)) for qi in range(seq)], axis=0)
    r_id = lax.broadcasted_iota(jnp.int32, (rows, D_MODEL), 0)
    head_lane = (r_id % N_HEADS) == (lax.broadcasted_iota(jnp.int32, (rows, D_MODEL), 1) // HEAD_DIM)
    q_bd = jnp.where(head_lane, q_rep, 0.0)
    q_bd_b = q_bd.astype(BF16)

    r_col = lax.broadcasted_iota(jnp.int32, (rows, 1), 0)
    t_pos = (past_len + r_col // N_HEADS).astype(F32)
    slope = slope_ref[...]
    j_id = lax.broadcasted_iota(jnp.int32, (rows, MOBA_BLOCK), 1).astype(F32)

    gates, scores = [], []
    for n in range(n_past):
        ks = [k_pages[n * pages_per_block + j][...] for j in range(pages_per_block)]
        k_sum = ks[0].sum(axis=0, keepdims=True)
        for kj in ks[1:]:
            k_sum = k_sum + kj.sum(axis=0, keepdims=True)
        gates.append(jnp.sum(q_bd * (k_sum * (1.0 / MOBA_BLOCK)), axis=1, keepdims=True))
        s = jnp.concatenate([_dot_nt(q_bd_b, kj.astype(BF16)) for kj in ks], axis=1)
        scores.append(s * scale - slope * (t_pos - (n * MOBA_BLOCK + j_id)))

    selected = []
    for n in range(n_past):
        rank = jnp.zeros((rows, 1), jnp.int32)
        for mth in range(n_past):
            if mth == n:
                continue
            ahead = (gates[mth] >= gates[n]) if mth < n else (gates[mth] > gates[n])
            rank = rank + ahead.astype(jnp.int32)
        selected.append(rank < MOBA_TOPK)

    kn = kn_ref[...]
    vn = vn_ref[...]
    q_idx = r_col // N_HEADS
    own_s = []
    for j in range(seq):
        sj = jnp.sum(q_bd * kn[j:j + 1, :], axis=1, keepdims=True) * scale
        sj = sj - slope * (t_pos - float(past_len + j))
        own_s.append(jnp.where(q_idx >= j, sj, -jnp.inf))

    m = own_s[0]
    for sj in own_s[1:]:
        m = jnp.maximum(m, sj)
    for n in range(n_past):
        m = jnp.maximum(m, jnp.where(selected[n], jnp.max(scores[n], axis=1, keepdims=True), -jnp.inf))

    l = jnp.zeros((rows, 1), F32)
    acc = jnp.zeros((rows, D_MODEL), F32)
    for j in range(seq):
        pj = jnp.exp(own_s[j] - m)
        l = l + pj
        acc = acc + pj * vn[j:j + 1, :]
    for n in range(n_past):
        p = jnp.where(selected[n], jnp.exp(scores[n] - m), 0.0)
        l = l + jnp.sum(p, axis=1, keepdims=True)
        p = p.astype(BF16)
        for j in range(pages_per_block):
            acc = acc + _dot(p[:, j * PAGE_SIZE:(j + 1) * PAGE_SIZE],
                             v_pages[n * pages_per_block + j][...].astype(BF16))

    out_bd = jnp.where(head_lane, acc / l, 0.0)
    for qi in range(seq):
        o_ref[qi:qi + 1, :] = jnp.sum(out_bd[qi * N_HEADS:(qi + 1) * N_HEADS, :], axis=0,
                                      keepdims=True).astype(o_ref.dtype)


def _moba_sample(page_table, q3, k_new3, v_new3, cache_k3, cache_v3):
    n_seq, seq, _ = q3.shape
    n_pages = page_table.shape[1]
    past_len = n_pages * PAGE_SIZE
    assert past_len % MOBA_BLOCK == 0 and seq <= MOBA_BLOCK
    assert past_len // MOBA_BLOCK >= MOBA_TOPK
    rows = seq * N_HEADS
    slopes = jnp.exp2(-ALIBI_MAX_EXP * jnp.arange(1, N_HEADS + 1, dtype=F32) / N_HEADS)
    slope_col = jnp.tile(slopes, seq).reshape(rows, 1)
    seq_spec = pl.BlockSpec((None, seq, D_MODEL), lambda b, pt: (b, 0, 0))

    def page_spec(p):
        return pl.BlockSpec((None, PAGE_SIZE, D_MODEL), lambda b, pt: (pt[b, p], 0, 0))

    grid_spec = pltpu.PrefetchScalarGridSpec(
        num_scalar_prefetch=1,
        grid=(n_seq,),
        in_specs=([seq_spec, seq_spec, seq_spec, pl.BlockSpec((rows, 1), lambda b, pt: (0, 0))]
                  + [page_spec(p) for p in range(n_pages)]
                  + [page_spec(p) for p in range(n_pages)]),
        out_specs=seq_spec,
    )
    return pl.pallas_call(
        functools.partial(_moba_sample_kernel, seq, past_len, n_pages),
        grid_spec=grid_spec,
        out_shape=jax.ShapeDtypeStruct((n_seq, seq, D_MODEL), F32),
        compiler_params=_compiler_params(1),
        name="moba_sample",
    )(page_table, q3, k_new3, v_new3, slope_col,
      *([cache_k3] * n_pages), *([cache_v3] * n_pages))


def _bf16_floor(v):
    bits = lax.bitcast_convert_type(v, jnp.uint32) & jnp.uint32(0xFFFF0000)
    return lax.bitcast_convert_type(bits, F32)


def _split3(v):
    hi = _bf16_floor(v)
    r1 = v - hi
    mid = _bf16_floor(r1)
    return [hi, mid, r1 - mid]


def _moba_aux(seq):
    assert seq // MOBA_BLOCK <= N_BLOCK_COLS and N_BLOCK_COLS + 6 <= HEAD_DIM
    slopes = jnp.exp2(-ALIBI_MAX_EXP * jnp.arange(1, N_HEADS + 1, dtype=F32) / N_HEADS)
    pos = jnp.arange(seq, dtype=F32)
    sp = slopes[None, :] * pos[:, None]
    ones = jnp.ones((seq, N_HEADS), F32)
    onehot = (jnp.arange(seq)[:, None] // MOBA_BLOCK == jnp.arange(N_BLOCK_COLS)[None, :]).astype(F32)
    onehot = jnp.broadcast_to(onehot[:, None, :], (seq, N_HEADS, N_BLOCK_COLS))
    pad = jnp.zeros((seq, N_HEADS, HEAD_DIM - N_BLOCK_COLS - 6), F32)
    k_cols = jnp.stack(_split3(sp) + [ones] * 3, axis=-1)
    q_cols = jnp.stack([ones] * 3 + _split3(-sp), axis=-1)
    k_aux = jnp.concatenate([onehot, k_cols, pad], axis=-1).reshape(seq, D_MODEL).astype(BF16)
    q_aux = jnp.concatenate([jnp.zeros_like(onehot), q_cols, pad], axis=-1).reshape(seq, D_MODEL)
    return k_aux, q_aux


def _gate_means(kmean, batch):
    blocks = kmean.shape[0] // batch
    km = kmean.reshape(batch, blocks, N_HEADS, HEAD_DIM).transpose(0, 2, 1, 3)
    km = jnp.pad(km, ((0, 0), (0, 0), (0, LANES - blocks), (0, 0)))
    hi = _bf16_floor(km)
    lo = km - hi
    return jnp.concatenate([hi, hi, lo], axis=-1).astype(BF16)


def kernel(x_prompt, x_sample, state_conv, cache_k, cache_v, cache_mem_k, cache_mem_v, page_table,
           mem_prompt, g_mix, w_in_a, conv_w, w_out_a, g_kv, w_kv, w_in_b, w_out_b,
           g_mem, w_mem_kv, g_mlp, w_up, w_down, g_final):
    batch, seq, _ = x_prompt.shape
    n_seq, dec_seq, _ = x_sample.shape
    n_phys = cache_k.shape[0]
    assert seq % TOKEN_TILE == 0 and (n_seq * dec_seq) % TOKEN_TILE == 0
    assert TOKEN_TILE % MOBA_BLOCK == 0 and n_seq % MEM_BATCH_TILE == 0

    w_in_a_b, w_out_a_b, w_kv_b = w_in_a.astype(BF16), w_out_a.astype(BF16), w_kv.astype(BF16)
    w_in_b_b, w_out_b_b = w_in_b.astype(BF16), w_out_b.astype(BF16)
    w_mem_kv_b, w_up_b, w_down_b = w_mem_kv.astype(BF16), w_up.astype(BF16), w_down.astype(BF16)

    mem_k, mem_v, mem_kb, mem_vb = _mem_kv(mem_prompt.reshape(batch * N_MEM, D_MODEL), g_mem, w_mem_kv_b)
    cache_mem_k4 = cache_mem_k.reshape(DEPTH, n_seq, N_MEM, D_MODEL)
    cache_mem_v4 = cache_mem_v.reshape(DEPTH, n_seq, N_MEM, D_MODEL)
    cache_k3 = cache_k.reshape(n_phys, PAGE_SIZE, D_MODEL)
    cache_v3 = cache_v.reshape(n_phys, PAGE_SIZE, D_MODEL)
    k_aux, q_aux = _moba_aux(seq)

    xp = x_prompt.reshape(batch * seq, D_MODEL)
    xs = x_sample.reshape(n_seq * dec_seq, D_MODEL)
    conv_p, conv_s = [], []
    for layer in range(DEPTH):
        if layer < N_A_LAYERS:
            tok_p, qm_p, ulast = _a_front_prompt(xp, batch, g_mix[layer], w_in_a_b[layer], conv_w[layer])
            conv_p.append(ulast.reshape(batch, -1, SUBLANES, D_MODEL)[:, -1, SUBLANES - (CONV_WIDTH - 1):])
            tok_s, qm_s, u_s = _a_front_sample(xs, dec_seq, g_mix[layer], w_in_a_b[layer], conv_w[layer],
                                               state_conv[layer])
            conv_s.append(u_s.reshape(n_seq, dec_seq, D_MODEL)[:, dec_seq - (CONV_WIDTH - 1):])
            w_out = w_out_a_b[layer]
        else:
            bi = layer - N_A_LAYERS
            qaug, qm_p = _b_front_prompt(xp, batch, g_mix[layer], w_in_b_b[bi], km3, q_aux)
            tok_p = _moba_prompt(qaug, kaug, vb_p, batch)
            q_s, qm_s = _b_front_sample(xs, g_mix[layer], w_in_b_b[bi])
            tok_s = _moba_sample(page_table, q_s.reshape(n_seq, dec_seq, D_MODEL), k_s3, v_s3,
                                 cache_k3, cache_v3).reshape(n_seq * dec_seq, D_MODEL)
            w_out = w_out_b_b[bi]
        mo_p = _mem_attn_prompt(qm_p, batch, mem_kb[layer], mem_vb[layer])
        mo_s = _mem_attn_sample(qm_s.reshape(n_seq, dec_seq, D_MODEL), layer, cache_mem_k4,
                                cache_mem_v4).reshape(n_seq * dec_seq, D_MODEL)
        last = layer == DEPTH - 1
        xp = _post(xp, tok_p, mo_p, w_out, g_mlp[layer], w_up_b[layer], w_down_b[layer], g_final, last)
        xs = _post(xs, tok_s, mo_s, w_out, g_mlp[layer], w_up_b[layer], w_down_b[layer], g_final, last)
        if layer == N_A_LAYERS - 1:
            k_p, v_p, kaug, vb_p, kmean = _kv_prompt(xp, g_kv, w_kv_b, k_aux)
            km3 = _gate_means(kmean, batch)
            k_s, v_s = _kv_sample(xs, g_kv, w_kv_b)
            k_s3 = k_s.reshape(n_seq, dec_seq, D_MODEL)
            v_s3 = v_s.reshape(n_seq, dec_seq, D_MODEL)

    mem_shape = (DEPTH, batch, N_MEM, MEM_HEADS, MEM_HEAD_DIM)
    return (xp.reshape(batch, seq, D_MODEL),
            xs.reshape(n_seq, dec_seq, D_MODEL),
            jnp.stack(conv_p),
            jnp.stack(conv_s),
            k_p.reshape(batch, seq, N_HEADS, HEAD_DIM),
            v_p.reshape(batch, seq, N_HEADS, HEAD_DIM),
            k_s.reshape(n_seq, dec_seq, N_HEADS, HEAD_DIM),
            v_s.reshape(n_seq, dec_seq, N_HEADS, HEAD_DIM),
            mem_k.reshape(mem_shape),
            mem_v.reshape(mem_shape))
```

```python
import functools

import jax
import jax.numpy as jnp
from jax import lax
from jax.experimental import pallas as pl
from jax.experimental.pallas import tpu as pltpu

F32 = jnp.float32
BF16 = jnp.bfloat16

D_MODEL = 1024
DEPTH = 4
N_A_LAYERS = DEPTH // 2
CONV_WIDTH = 3
N_HEADS = 8
HEAD_DIM = D_MODEL // N_HEADS
MOBA_BLOCK = 256
MOBA_TOPK = 3
N_MEM = 256
MEM_HEADS = 4
MEM_HEAD_DIM = D_MODEL // MEM_HEADS
D_FF = 4 * D_MODEL
ALIBI_MAX_EXP = 8.0
EPS = 1e-5
PAGE_SIZE = 128

LANES = 128
SUBLANES = 8
VMEM_LIMIT_BYTES = 56 * 1024 * 1024

TOKEN_TILE = 512
AUG_WIDTH = 2 * HEAD_DIM
MOBA_CHUNK = 2 * MOBA_BLOCK
N_BLOCK_COLS = 40
MASK_NEG = -(2.0 ** 100)
MEM_BATCH_TILE = 4


def _compiler_params(n_grid_dims):
    return pltpu.CompilerParams(
        dimension_semantics=("arbitrary",) * n_grid_dims,
        vmem_limit_bytes=VMEM_LIMIT_BYTES)


def _resident(shape):
    zeros = (0,) * len(shape)
    return pl.BlockSpec(shape, lambda *_: zeros, pipeline_mode=pl.Buffered(1))


def _rms(x, g):
    return x * lax.rsqrt(jnp.mean(x * x, axis=-1, keepdims=True) + EPS) * g


def _dot(a, b):
    return jnp.dot(a, b, preferred_element_type=F32)


def _dot_nt(a, b):
    return lax.dot_general(a, b, (((1,), (1,)), ((), ())), preferred_element_type=F32)


def _softmax_pv(s, v):
    m = jnp.max(s, axis=-1, keepdims=True)
    p = jnp.exp(s - m)
    l = jnp.sum(p, axis=-1, keepdims=True)
    return _dot(p.astype(BF16), v) / l


def _mem_kv_kernel(mem_ref, g_ref, w_ref, k_ref, v_ref, kb_ref, vb_ref):
    hn = _rms(mem_ref[...], g_ref[...]).astype(BF16)
    kv = _dot(hn, w_ref[...])
    k = kv[:, :D_MODEL]
    v = kv[:, D_MODEL:]
    k_ref[...] = k
    v_ref[...] = v
    kb_ref[...] = k.astype(BF16)
    vb_ref[...] = v.astype(BF16)


def _mem_kv(mem2d, g_mem, w_mem_kv_b):
    rows = mem2d.shape[0]
    out_f = jax.ShapeDtypeStruct((DEPTH, rows, D_MODEL), F32)
    out_b = jax.ShapeDtypeStruct((DEPTH, rows, D_MODEL), BF16)
    blk = pl.BlockSpec((None, rows, D_MODEL), lambda l: (l, 0, 0))
    return pl.pallas_call(
        _mem_kv_kernel,
        grid=(DEPTH,),
        in_specs=[
            pl.BlockSpec((rows, D_MODEL), lambda l: (0, 0)),
            pl.BlockSpec((None, 1, D_MODEL), lambda l: (l, 0, 0)),
            pl.BlockSpec((None, D_MODEL, 2 * D_MODEL), lambda l: (l, 0, 0)),
        ],
        out_specs=[blk, blk, blk, blk],
        out_shape=[out_f, out_f, out_b, out_b],
        compiler_params=_compiler_params(1),
        name="mem_kv",
    )(mem2d, g_mem.reshape(DEPTH, 1, D_MODEL), w_mem_kv_b)


def _conv_gate(h, w_ref, cw_ref, fix_prev):
    cg = _dot(h, w_ref[:, D_MODEL:2 * D_MODEL])
    hv = _dot(h, w_ref[:, 2 * D_MODEL:3 * D_MODEL])
    u = cg * hv
    prev1, prev2 = fix_prev(pltpu.roll(u, 1, 0), pltpu.roll(u, 2, 0))
    cw = cw_ref[...]
    z = cw[0:1, :] * prev2 + cw[1:2, :] * prev1 + cw[2:3, :] * u
    bg = _dot(h, w_ref[:, 0:D_MODEL])
    qm = _dot(h, w_ref[:, 3 * D_MODEL:4 * D_MODEL])
    return bg * z, qm, u


def _a_front_prompt_kernel(x_ref, g_ref, w_ref, cw_ref, tok_ref, qm_ref, ulast_ref, carry_ref):
    @pl.when(pl.program_id(1) == 0)
    def _():
        carry_ref[...] = jnp.zeros_like(carry_ref)

    h = _rms(x_ref[...], g_ref[...]).astype(BF16)
    carry = carry_ref[...]

    def fix_prev(prev1, prev2):
        row = lax.broadcasted_iota(jnp.int32, prev1.shape, 0)
        c_m1 = carry[SUBLANES - 1:SUBLANES, :]
        c_m2 = carry[SUBLANES - 2:SUBLANES - 1, :]
        prev1 = jnp.where(row < 1, c_m1, prev1)
        prev2 = jnp.where(row < 1, c_m2, jnp.where(row < 2, c_m1, prev2))
        return prev1, prev2

    tok, qm, u = _conv_gate(h, w_ref, cw_ref, fix_prev)
    tok_ref[...] = tok.astype(BF16)
    qm_ref[...] = qm.astype(BF16)
    tail = u[TOKEN_TILE - SUBLANES:, :]
    carry_ref[...] = tail
    ulast_ref[...] = tail


def _a_front_prompt(x2d, batch, g, w_b, conv_w):
    rows = x2d.shape[0]
    tiles = rows // batch // TOKEN_TILE
    tile_spec = pl.BlockSpec((TOKEN_TILE, D_MODEL), lambda b, i: (b * tiles + i, 0))
    return pl.pallas_call(
        _a_front_prompt_kernel,
        grid=(batch, tiles),
        in_specs=[
            tile_spec,
            _resident((1, D_MODEL)),
            _resident((D_MODEL, 4 * D_MODEL)),
            _resident((CONV_WIDTH, D_MODEL)),
        ],
        out_specs=[
            tile_spec,
            tile_spec,
            pl.BlockSpec((SUBLANES, D_MODEL), lambda b, i: (b * tiles + i, 0)),
        ],
        out_shape=[
            jax.ShapeDtypeStruct((rows, D_MODEL), BF16),
            jax.ShapeDtypeStruct((rows, D_MODEL), BF16),
            jax.ShapeDtypeStruct((batch * tiles * SUBLANES, D_MODEL), F32),
        ],
        scratch_shapes=[pltpu.VMEM((SUBLANES, D_MODEL), F32)],
        compiler_params=_compiler_params(2),
        name="a_front_prompt",
    )(x2d, g.reshape(1, D_MODEL), w_b, conv_w)


def _a_front_sample_kernel(seq, x_ref, g_ref, w_ref, cw_ref, s1_ref, s2_ref, tok_ref, qm_ref, u_ref):
    h = _rms(x_ref[...], g_ref[...]).astype(BF16)

    def fix_prev(prev1, prev2):
        pos = lax.broadcasted_iota(jnp.int32, prev1.shape, 0) % seq
        return (jnp.where(pos >= 1, prev1, s1_ref[...]),
                jnp.where(pos >= 2, prev2, s2_ref[...]))

    tok, qm, u = _conv_gate(h, w_ref, cw_ref, fix_prev)
    tok_ref[...] = tok.astype(BF16)
    qm_ref[...] = qm
    u_ref[...] = u


def _a_front_sample(x2d, seq, g, w_b, conv_w, state):
    rows = x2d.shape[0]
    n_seq = rows // seq
    assert seq >= CONV_WIDTH - 1
    zeros = jnp.zeros((n_seq, seq - 1, D_MODEL), F32)
    s1 = jnp.concatenate([state[:, 1:2], zeros], axis=1).reshape(rows, D_MODEL)
    s2 = jnp.concatenate([state, zeros[:, 1:]], axis=1).reshape(rows, D_MODEL)
    full = pl.BlockSpec((rows, D_MODEL), lambda i: (0, 0))
    return pl.pallas_call(
        functools.partial(_a_front_sample_kernel, seq),
        grid=(1,),
        in_specs=[full, _resident((1, D_MODEL)), _resident((D_MODEL, 4 * D_MODEL)),
                  _resident((CONV_WIDTH, D_MODEL)), full, full],
        out_specs=[full, full, full],
        out_shape=[
            jax.ShapeDtypeStruct((rows, D_MODEL), BF16),
            jax.ShapeDtypeStruct((rows, D_MODEL), F32),
            jax.ShapeDtypeStruct((rows, D_MODEL), F32),
        ],
        compiler_params=_compiler_params(1),
        name="a_front_sample",
    )(x2d, g.reshape(1, D_MODEL), w_b, conv_w, s1, s2)


def _mem_attn_prompt_kernel(q_ref, k_ref, v_ref, o_ref):
    scale = MEM_HEAD_DIM ** -0.5
    for hd in range(MEM_HEADS):
        sl = slice(hd * MEM_HEAD_DIM, (hd + 1) * MEM_HEAD_DIM)
        s = _dot_nt(q_ref[:, sl], k_ref[:, sl]) * scale
        o_ref[:, sl] = _softmax_pv(s, v_ref[:, sl]).astype(o_ref.dtype)


def _mem_attn_prompt(qm, batch, k_b, v_b):
    rows = qm.shape[0]
    tiles = rows // batch // TOKEN_TILE
    tile_spec = pl.BlockSpec((TOKEN_TILE, D_MODEL), lambda b, i: (b * tiles + i, 0))
    kv_spec = pl.BlockSpec((N_MEM, D_MODEL), lambda b, i: (b, 0))
    return pl.pallas_call(
        _mem_attn_prompt_kernel,
        grid=(batch, tiles),
        in_specs=[tile_spec, kv_spec, kv_spec],
        out_specs=tile_spec,
        out_shape=jax.ShapeDtypeStruct((rows, D_MODEL), BF16),
        compiler_params=_compiler_params(2),
        name="mem_attn_prompt",
    )(qm, k_b, v_b)


def _mem_attn_sample_kernel(seq, q_ref, k_ref, v_ref, o_ref):
    scale = MEM_HEAD_DIM ** -0.5
    pad = jnp.zeros((SUBLANES - seq, MEM_HEAD_DIM), F32)
    for bb in range(MEM_BATCH_TILE):
        for hd in range(MEM_HEADS):
            sl = slice(hd * MEM_HEAD_DIM, (hd + 1) * MEM_HEAD_DIM)
            q = jnp.concatenate([q_ref[bb, :, sl], pad], axis=0).astype(BF16)
            s = _dot_nt(q, k_ref[bb, :, sl].astype(BF16)) * scale
            o = _softmax_pv(s, v_ref[bb, :, sl].astype(BF16))
            o_ref[bb, :, sl] = o[:seq, :].astype(o_ref.dtype)


def _mem_attn_sample(qm3, layer, cache_k4, cache_v4):
    n_seq, seq, _ = qm3.shape
    q_spec = pl.BlockSpec((MEM_BATCH_TILE, seq, D_MODEL), lambda i: (i, 0, 0))
    kv_spec = pl.BlockSpec((None, MEM_BATCH_TILE, N_MEM, D_MODEL), lambda i: (layer, i, 0, 0))
    return pl.pallas_call(
        functools.partial(_mem_attn_sample_kernel, seq),
        grid=(n_seq // MEM_BATCH_TILE,),
        in_specs=[q_spec, kv_spec, kv_spec],
        out_specs=q_spec,
        out_shape=jax.ShapeDtypeStruct((n_seq, seq, D_MODEL), F32),
        compiler_params=_compiler_params(1),
        name="mem_attn_sample",
    )(qm3, cache_k4, cache_v4)


def _post_kernel(final_norm, x_ref, tok_ref, mo_ref, wo_ref, g_ref, wu_ref, wd_ref, gf_ref, o_ref):
    o_ref[...] = (x_ref[...]
                  + _dot(tok_ref[...].astype(BF16), wo_ref[0:D_MODEL, :])
                  + _dot(mo_ref[...].astype(BF16), wo_ref[D_MODEL:2 * D_MODEL, :]))
    x = o_ref[...]
    h = _rms(x, g_ref[...]).astype(BF16)
    for c in range(D_FF // D_MODEL):
        sl = slice(c * D_MODEL, (c + 1) * D_MODEL)
        a = jnp.square(jnp.maximum(_dot(h, wu_ref[:, sl]), 0.0)).astype(BF16)
        x = x + _dot(a, wd_ref[sl, :])
    o_ref[...] = x
    if final_norm:
        o_ref[...] = _rms(o_ref[...], gf_ref[...])


def _post(x2d, tok, mo, wo_b, g, wu_b, wd_b, g_final, final_norm):
    rows = x2d.shape[0]
    tile_spec = pl.BlockSpec((TOKEN_TILE, D_MODEL), lambda i: (i, 0))
    return pl.pallas_call(
        functools.partial(_post_kernel, final_norm),
        grid=(rows // TOKEN_TILE,),
        in_specs=[
            tile_spec, tile_spec, tile_spec,
            _resident((2 * D_MODEL, D_MODEL)),
            _resident((1, D_MODEL)),
            _resident((D_MODEL, D_FF)),
            _resident((D_FF, D_MODEL)),
            _resident((1, D_MODEL)),
        ],
        out_specs=tile_spec,
        out_shape=jax.ShapeDtypeStruct((rows, D_MODEL), F32),
        compiler_params=_compiler_params(1),
        name="post_final" if final_norm else "post",
    )(x2d, tok, mo, wo_b, g.reshape(1, D_MODEL), wu_b, wd_b, g_final.reshape(1, D_MODEL))


def _kv_prompt_kernel(x_ref, g_ref, w_ref, kaux_ref, k_ref, v_ref, kaug_ref, vb_ref, kmean_ref):
    h = _rms(x_ref[...], g_ref[...]).astype(BF16)
    kv = _dot(h, w_ref[...])
    k = kv[:, :D_MODEL]
    v = kv[:, D_MODEL:]
    k_ref[...] = k
    v_ref[...] = v
    vb_ref[...] = v.astype(BF16)
    kmean_ref[...] = jnp.sum(k, axis=0, keepdims=True) * (1.0 / MOBA_BLOCK)
    for hd in range(N_HEADS):
        src = slice(hd * HEAD_DIM, (hd + 1) * HEAD_DIM)
        kaug_ref[:, hd * AUG_WIDTH:hd * AUG_WIDTH + HEAD_DIM] = k[:, src].astype(BF16)
        kaug_ref[:, hd * AUG_WIDTH + HEAD_DIM:(hd + 1) * AUG_WIDTH] = kaux_ref[:, src]


def _kv_prompt(x2d, g, w_b, kaux):
    rows = x2d.shape[0]
    n_blocks = rows // MOBA_BLOCK
    seq_blocks = kaux.shape[0] // MOBA_BLOCK
    tile = lambda width: pl.BlockSpec((MOBA_BLOCK, width), lambda i: (i, 0))
    return pl.pallas_call(
        _kv_prompt_kernel,
        grid=(n_blocks,),
        in_specs=[
            tile(D_MODEL),
            _resident((1, D_MODEL)),
            _resident((D_MODEL, 2 * D_MODEL)),
            pl.BlockSpec((MOBA_BLOCK, D_MODEL), lambda i: (i % seq_blocks, 0)),
        ],
        out_specs=[
            tile(D_MODEL), tile(D_MODEL), tile(N_HEADS * AUG_WIDTH), tile(D_MODEL),
            pl.BlockSpec((None, 1, D_MODEL), lambda i: (i, 0, 0)),
        ],
        out_shape=[
            jax.ShapeDtypeStruct((rows, D_MODEL), F32),
            jax.ShapeDtypeStruct((rows, D_MODEL), F32),
            jax.ShapeDtypeStruct((rows, N_HEADS * AUG_WIDTH), BF16),
            jax.ShapeDtypeStruct((rows, D_MODEL), BF16),
            jax.ShapeDtypeStruct((n_blocks, 1, D_MODEL), F32),
        ],
        compiler_params=_compiler_params(1),
        name="kv_prompt",
    )(x2d, g.reshape(1, D_MODEL), w_b, kaux)


def _kv_sample_kernel(x_ref, g_ref, w_ref, k_ref, v_ref):
    h = _rms(x_ref[...], g_ref[...]).astype(BF16)
    kv = _dot(h, w_ref[...])
    k_ref[...] = kv[:, :D_MODEL]
    v_ref[...] = kv[:, D_MODEL:]


def _kv_sample(x2d, g, w_b):
    rows = x2d.shape[0]
    tile_spec = pl.BlockSpec((TOKEN_TILE, D_MODEL), lambda i: (i, 0))
    out = jax.ShapeDtypeStruct((rows, D_MODEL), F32)
    return pl.pallas_call(
        _kv_sample_kernel,
        grid=(rows // TOKEN_TILE,),
        in_specs=[tile_spec, _resident((1, D_MODEL)), _resident((D_MODEL, 2 * D_MODEL))],
        out_specs=[tile_spec, tile_spec],
        out_shape=[out, out],
        compiler_params=_compiler_params(1),
        name="kv_sample",
    )(x2d, g.reshape(1, D_MODEL), w_b)


def _b_front_prompt_kernel(tiles, x_ref, g_ref, w_ref, km_ref, qaux_ref, qaug_ref, qm_ref):
    h = _rms(x_ref[...], g_ref[...]).astype(BF16)
    qm_ref[...] = _dot(h, w_ref[:, D_MODEL:]).astype(BF16)
    q = _dot(h, w_ref[:, :D_MODEL])
    scale = HEAD_DIM ** -0.5
    t0 = (pl.program_id(0) % tiles) * TOKEN_TILE
    shape = (LANES, TOKEN_TILE)
    blk = lax.broadcasted_iota(jnp.int32, shape, 0)
    own = (t0 + lax.broadcasted_iota(jnp.int32, shape, 1)) // MOBA_BLOCK
    eligible = blk < own
    for hd in range(N_HEADS):
        qh = q[:, hd * HEAD_DIM:(hd + 1) * HEAD_DIM]
        q_hi = qh.astype(BF16)
        q_lo = (qh - q_hi.astype(F32)).astype(BF16)
        gate = _dot_nt(km_ref[hd], jnp.concatenate([q_hi, q_lo, q_hi], axis=1))
        gate = jnp.where(eligible, gate, -jnp.inf)
        masked = eligible
        for _ in range(MOBA_TOPK):
            best = jnp.max(gate, axis=0, keepdims=True)
            is_best = (gate == best) & (best > -jnp.inf)
            first = jnp.min(jnp.where(is_best, blk, LANES), axis=0, keepdims=True)
            pick = blk == first
            masked = masked & jnp.logical_not(pick)
            gate = jnp.where(pick, -jnp.inf, gate)
        neg = jnp.where(masked, MASK_NEG, 0.0)
        aux = neg.T + qaux_ref[:, hd * HEAD_DIM:(hd + 1) * HEAD_DIM]
        qaug_ref[:, hd * AUG_WIDTH:hd * AUG_WIDTH + HEAD_DIM] = (qh * scale).astype(BF16)
        qaug_ref[:, hd * AUG_WIDTH + HEAD_DIM:(hd + 1) * AUG_WIDTH] = aux.astype(BF16)


def _b_front_prompt(x2d, batch, g, w_b, km3, qaux):
    rows = x2d.shape[0]
    tiles = rows // batch // TOKEN_TILE
    tile = lambda width: pl.BlockSpec((TOKEN_TILE, width), lambda i: (i, 0))
    return pl.pallas_call(
        functools.partial(_b_front_prompt_kernel, tiles),
        grid=(batch * tiles,),
        in_specs=[
            tile(D_MODEL),
            _resident((1, D_MODEL)),
            _resident((D_MODEL, 2 * D_MODEL)),
            pl.BlockSpec((None, N_HEADS, LANES, 3 * HEAD_DIM), lambda i: (i // tiles, 0, 0, 0)),
            pl.BlockSpec((TOKEN_TILE, D_MODEL), lambda i: (i % tiles, 0)),
        ],
        out_specs=[tile(N_HEADS * AUG_WIDTH), tile(D_MODEL)],
        out_shape=[
            jax.ShapeDtypeStruct((rows, N_HEADS * AUG_WIDTH), BF16),
            jax.ShapeDtypeStruct((rows, D_MODEL), BF16),
        ],
        compiler_params=_compiler_params(1),
        name="b_front_prompt",
    )(x2d, g.reshape(1, D_MODEL), w_b, km3, qaux)


def _b_front_sample_kernel(x_ref, g_ref, w_ref, q_ref, qm_ref):
    h = _rms(x_ref[...], g_ref[...]).astype(BF16)
    q_ref[...] = _dot(h, w_ref[:, :D_MODEL])
    qm_ref[...] = _dot(h, w_ref[:, D_MODEL:])


def _b_front_sample(x2d, g, w_b):
    rows = x2d.shape[0]
    tile_spec = pl.BlockSpec((TOKEN_TILE, D_MODEL), lambda i: (i, 0))
    return pl.pallas_call(
        _b_front_sample_kernel,
        grid=(rows // TOKEN_TILE,),
        in_specs=[tile_spec, _resident((1, D_MODEL)), _resident((D_MODEL, 2 * D_MODEL))],
        out_specs=[tile_spec, tile_spec],
        out_shape=[jax.ShapeDtypeStruct((rows, D_MODEL), F32),
                   jax.ShapeDtypeStruct((rows, D_MODEL), F32)],
        compiler_params=_compiler_params(1),
        name="b_front_sample",
    )(x2d, g.reshape(1, D_MODEL), w_b)


def _moba_prompt_kernel(q_ref, k_ref, v_ref, o_ref):
    j = pl.program_id(2)
    q = q_ref[...]

    def scores(c):
        start = pl.multiple_of(c * MOBA_CHUNK, MOBA_CHUNK)
        return _dot_nt(q, k_ref[pl.ds(start, MOBA_CHUNK), :]), start

    def update(state, s, start):
        m, l, acc = state
        m_new = jnp.maximum(m, jnp.max(s, axis=-1, keepdims=True))
        alpha = jnp.exp(m - m_new)
        p = jnp.exp(s - m_new)
        l = alpha * l + jnp.sum(p, axis=-1, keepdims=True)
        acc = alpha * acc + _dot(p.astype(BF16), v_ref[pl.ds(start, MOBA_CHUNK), :])
        return m_new, l, acc

    s, start = scores(j)
    row = lax.broadcasted_iota(jnp.int32, s.shape, 0)
    col = lax.broadcasted_iota(jnp.int32, s.shape, 1)
    s = jnp.where(col <= row, s, MASK_NEG)
    m = jnp.max(s, axis=-1, keepdims=True)
    p = jnp.exp(s - m)
    state_a = (m, jnp.sum(p, axis=-1, keepdims=True),
               _dot(p.astype(BF16), v_ref[pl.ds(start, MOBA_CHUNK), :]))
    state_b = (jnp.full_like(m, MASK_NEG), jnp.zeros_like(m), jnp.zeros_like(state_a[2]))

    def pair(t, carry):
        sa, start_a = scores(2 * t)
        sb, start_b = scores(2 * t + 1)
        return update(carry[0], sa, start_a), update(carry[1], sb, start_b)

    state_a, state_b = lax.fori_loop(0, lax.shift_right_logical(j, 1), pair, (state_a, state_b))
    state_a = lax.cond((j & 1) == 1, lambda st: update(st, *scores(j - 1)), lambda st: st, state_a)

    m = jnp.maximum(state_a[0], state_b[0])
    wa = jnp.exp(state_a[0] - m)
    wb = jnp.exp(state_b[0] - m)
    l = wa * state_a[1] + wb * state_b[1]
    o_ref[...] = ((wa * state_a[2] + wb * state_b[2]) / l).astype(o_ref.dtype)


def _moba_prompt(qaug, kaug, vb, batch):
    rows = qaug.shape[0]
    seq = rows // batch
    assert seq % MOBA_CHUNK == 0 and MOBA_CHUNK % MOBA_BLOCK == 0
    q_tiles = seq // MOBA_CHUNK
    return pl.pallas_call(
        _moba_prompt_kernel,
        grid=(batch, N_HEADS, q_tiles),
        in_specs=[
            pl.BlockSpec((MOBA_CHUNK, AUG_WIDTH), lambda b, h, j: (b * q_tiles + j, h)),
            pl.BlockSpec((seq, AUG_WIDTH), lambda b, h, j: (b, h)),
            pl.BlockSpec((seq, HEAD_DIM), lambda b, h, j: (b, h)),
        ],
        out_specs=pl.BlockSpec((MOBA_CHUNK, HEAD_DIM), lambda b, h, j: (b * q_tiles + j, h)),
        out_shape=jax.ShapeDtypeStruct((rows, D_MODEL), BF16),
        compiler_params=_compiler_params(3),
        name="moba_prompt",
    )(qaug, kaug, vb)


def _moba_sample_kernel(seq, past_len, n_pages, pt_ref, q_ref, kn_ref, vn_ref, slope_ref, *refs):
    del pt_ref
    k_pages = refs[:n_pages]
    v_pages = refs[n_pages:2 * n_pages]
    o_ref = refs[2 * n_pages]
    n_past = n_pages // 2
    rows = seq * N_HEADS
    cols = PAGE_SIZE * N_HEADS
    scale = HEAD_DIM ** -0.5

    q = q_ref[...]
    qb = q.astype(BF16)
    zq = jnp.zeros_like(qb)
    q_pair = jnp.concatenate([jnp.concatenate([qb, zq], axis=1),
                              jnp.concatenate([zq, qb], axis=1)], axis=0)

    def both(x):
        return jnp.concatenate([x, x], axis=0)

    slope = slope_ref[...]
    t_pos = (past_len + lax.broadcasted_iota(jnp.int32, (rows, 1), 0) // N_HEADS).astype(F32)
    r_id = lax.broadcasted_iota(jnp.int32, (2 * rows, cols), 0)
    c_id = lax.broadcasted_iota(jnp.int32, (2 * rows, cols), 1)
    same_head = (r_id % N_HEADS) == (c_id % N_HEADS)
    key_pos = ((r_id // rows) * PAGE_SIZE + c_id // N_HEADS).astype(F32)
    bias0 = jnp.where(same_head, -both(slope) * (both(t_pos) - key_pos), -jnp.inf)

    gates, scores = [], []
    for n in range(n_past):
        k0 = k_pages[2 * n][...]
        k1 = k_pages[2 * n + 1][...]
        k_mean = (jnp.sum(k0, axis=0) + jnp.sum(k1, axis=0)) * (1.0 / MOBA_BLOCK)
        gates.append(jnp.sum(q * jnp.concatenate([k_mean] * seq, axis=0), axis=1, keepdims=True))
        k_pair = jnp.concatenate([k0.reshape(cols, HEAD_DIM).astype(BF16),
                                  k1.reshape(cols, HEAD_DIM).astype(BF16)], axis=1)
        scores.append(_dot_nt(q_pair, k_pair) * scale + (bias0 + both(slope) * float(n * MOBA_BLOCK)))

    selected = []
    for n in range(n_past):
        rank = jnp.zeros((rows, 1), jnp.int32)
        for mth in range(n_past):
            if mth == n:
                continue
            ahead = (gates[mth] >= gates[n]) if mth < n else (gates[mth] > gates[n])
            rank = rank + ahead.astype(jnp.int32)
        selected.append(rank < MOBA_TOPK)

    r_own = lax.broadcasted_iota(jnp.int32, (rows, rows), 0)
    c_own = lax.broadcasted_iota(jnp.int32, (rows, rows), 1)
    own_ok = ((r_own % N_HEADS) == (c_own % N_HEADS)) & (c_own // N_HEADS <= r_own // N_HEADS)
    own_bias = -slope * (t_pos - (past_len + c_own // N_HEADS).astype(F32))
    s_own = jnp.where(own_ok, _dot_nt(qb, kn_ref[...].astype(BF16)) * scale + own_bias, -jnp.inf)

    m = jnp.max(s_own, axis=1, keepdims=True)
    for n in range(n_past):
        bm = jnp.max(scores[n], axis=1, keepdims=True)
        m = jnp.maximum(m, jnp.where(selected[n], jnp.maximum(bm[:rows], bm[rows:]), -jnp.inf))

    p_own = jnp.exp(s_own - m)
    l = jnp.sum(p_own, axis=1, keepdims=True)
    acc = _dot(p_own.astype(BF16), vn_ref[...].astype(BF16))
    for n in range(n_past):
        p = jnp.where(both(selected[n]), jnp.exp(scores[n] - both(m)), 0.0)
        ps = jnp.sum(p, axis=1, keepdims=True)
        l = l + ps[:rows] + ps[rows:]
        v_pair = jnp.concatenate([v_pages[2 * n][...].reshape(cols, HEAD_DIM).astype(BF16),
                                  v_pages[2 * n + 1][...].reshape(cols, HEAD_DIM).astype(BF16)], axis=1)
        pv = _dot(p.astype(BF16), v_pair)
        acc = acc + pv[:rows, :HEAD_DIM] + pv[rows:, HEAD_DIM:]
    o_ref[...] = acc / l


def _moba_sample(page_table, q3, k_new3, v_new3, cache_k, cache_v):
    n_seq, rows, _ = q3.shape
    seq = rows // N_HEADS
    n_pages = page_table.shape[1]
    past_len = n_pages * PAGE_SIZE
    assert MOBA_BLOCK == 2 * PAGE_SIZE and past_len % MOBA_BLOCK == 0 and seq <= MOBA_BLOCK
    assert past_len // MOBA_BLOCK >= MOBA_TOPK
    slopes = jnp.exp2(-ALIBI_MAX_EXP * jnp.arange(1, N_HEADS + 1, dtype=F32) / N_HEADS)
    slope_col = jnp.tile(slopes, seq).reshape(rows, 1)
    seq_spec = pl.BlockSpec((None, rows, HEAD_DIM), lambda b, pt: (b, 0, 0))

    def page_spec(p):
        return pl.BlockSpec((None, PAGE_SIZE, N_HEADS, HEAD_DIM), lambda b, pt: (pt[b, p], 0, 0, 0))

    grid_spec = pltpu.PrefetchScalarGridSpec(
        num_scalar_prefetch=1,
        grid=(n_seq,),
        in_specs=([seq_spec, seq_spec, seq_spec, pl.BlockSpec((rows, 1), lambda b, pt: (0, 0))]
                  + [page_spec(p) for p in range(n_pages)]
                  + [page_spec(p) for p in range(n_pages)]),
        out_specs=seq_spec,
    )
    return pl.pallas_call(
        functools.partial(_moba_sample_kernel, seq, past_len, n_pages),
        grid_spec=grid_spec,
        out_shape=jax.ShapeDtypeStruct((n_seq, rows, HEAD_DIM), F32),
        compiler_params=_compiler_params(1),
        name="moba_sample",
    )(page_table, q3, k_new3, v_new3, slope_col,
      *([cache_k] * n_pages), *([cache_v] * n_pages))


def _bf16_floor(v):
    bits = lax.bitcast_convert_type(v, jnp.uint32) & jnp.uint32(0xFFFF0000)
    return lax.bitcast_convert_type(bits, F32)


def _split3(v):
    hi = _bf16_floor(v)
    r1 = v - hi
    mid = _bf16_floor(r1)
    return [hi, mid, r1 - mid]


def _moba_aux(seq):
    assert seq // MOBA_BLOCK <= N_BLOCK_COLS and N_BLOCK_COLS + 6 <= HEAD_DIM
    slopes = jnp.exp2(-ALIBI_MAX_EXP * jnp.arange(1, N_HEADS + 1, dtype=F32) / N_HEADS)
    pos = jnp.arange(seq, dtype=F32)
    sp = slopes[None, :] * pos[:, None]
    ones = jnp.ones((seq, N_HEADS), F32)
    onehot = (jnp.arange(seq)[:, None] // MOBA_BLOCK == jnp.arange(N_BLOCK_COLS)[None, :]).astype(F32)
    onehot = jnp.broadcast_to(onehot[:, None, :], (seq, N_HEADS, N_BLOCK_COLS))
    pad = jnp.zeros((seq, N_HEADS, HEAD_DIM - N_BLOCK_COLS - 6), F32)
    k_cols = jnp.stack(_split3(sp) + [ones] * 3, axis=-1)
    q_cols = jnp.stack([ones] * 3 + _split3(-sp), axis=-1)
    k_aux = jnp.concatenate([onehot, k_cols, pad], axis=-1).reshape(seq, D_MODEL).astype(BF16)
    q_aux = jnp.concatenate([jnp.zeros_like(onehot), q_cols, pad], axis=-1).reshape(seq, D_MODEL)
    return k_aux, q_aux


def _gate_means(kmean, batch):
    blocks = kmean.shape[0] // batch
    km = kmean.reshape(batch, blocks, N_HEADS, HEAD_DIM).transpose(0, 2, 1, 3)
    km = jnp.pad(km, ((0, 0), (0, 0), (0, LANES - blocks), (0, 0)))
    hi = _bf16_floor(km)
    lo = km - hi
    return jnp.concatenate([hi, hi, lo], axis=-1).astype(BF16)


def kernel(x_prompt, x_sample, state_conv, cache_k, cache_v, cache_mem_k, cache_mem_v, page_table,
           mem_prompt, g_mix, w_in_a, conv_w, w_out_a, g_kv, w_kv, w_in_b, w_out_b,
           g_mem, w_mem_kv, g_mlp, w_up, w_down, g_final):
    batch, seq, _ = x_prompt.shape
    n_seq, dec_seq, _ = x_sample.shape
    assert seq % TOKEN_TILE == 0 and (n_seq * dec_seq) % TOKEN_TILE == 0
    assert TOKEN_TILE % MOBA_BLOCK == 0 and n_seq % MEM_BATCH_TILE == 0

    w_in_a_b, w_out_a_b, w_kv_b = w_in_a.astype(BF16), w_out_a.astype(BF16), w_kv.astype(BF16)
    w_in_b_b, w_out_b_b = w_in_b.astype(BF16), w_out_b.astype(BF16)
    w_mem_kv_b, w_up_b, w_down_b = w_mem_kv.astype(BF16), w_up.astype(BF16), w_down.astype(BF16)

    mem_k, mem_v, mem_kb, mem_vb = _mem_kv(mem_prompt.reshape(batch * N_MEM, D_MODEL), g_mem, w_mem_kv_b)
    cache_mem_k4 = cache_mem_k.reshape(DEPTH, n_seq, N_MEM, D_MODEL)
    cache_mem_v4 = cache_mem_v.reshape(DEPTH, n_seq, N_MEM, D_MODEL)
    k_aux, q_aux = _moba_aux(seq)

    xp = x_prompt.reshape(batch * seq, D_MODEL)
    xs = x_sample.reshape(n_seq * dec_seq, D_MODEL)
    conv_p, conv_s = [], []
    for layer in range(DEPTH):
        if layer < N_A_LAYERS:
            tok_p, qm_p, ulast = _a_front_prompt(xp, batch, g_mix[layer], w_in_a_b[layer], conv_w[layer])
            conv_p.append(ulast.reshape(batch, -1, SUBLANES, D_MODEL)[:, -1, SUBLANES - (CONV_WIDTH - 1):])
            tok_s, qm_s, u_s = _a_front_sample(xs, dec_seq, g_mix[layer], w_in_a_b[layer], conv_w[layer],
                                               state_conv[layer])
            conv_s.append(u_s.reshape(n_seq, dec_seq, D_MODEL)[:, dec_seq - (CONV_WIDTH - 1):])
            w_out = w_out_a_b[layer]
        else:
            bi = layer - N_A_LAYERS
            qaug, qm_p = _b_front_prompt(xp, batch, g_mix[layer], w_in_b_b[bi], km3, q_aux)
            tok_p = _moba_prompt(qaug, kaug, vb_p, batch)
            q_s, qm_s = _b_front_sample(xs, g_mix[layer], w_in_b_b[bi])
            tok_s = _moba_sample(page_table, q_s.reshape(n_seq, dec_seq * N_HEADS, HEAD_DIM), k_s3, v_s3,
                                 cache_k, cache_v).reshape(n_seq * dec_seq, D_MODEL)
            w_out = w_out_b_b[bi]
        mo_p = _mem_attn_prompt(qm_p, batch, mem_kb[layer], mem_vb[layer])
        mo_s = _mem_attn_sample(qm_s.reshape(n_seq, dec_seq, D_MODEL), layer, cache_mem_k4,
                                cache_mem_v4).reshape(n_seq * dec_seq, D_MODEL)
        last = layer == DEPTH - 1
        xp = _post(xp, tok_p, mo_p, w_out, g_mlp[layer], w_up_b[layer], w_down_b[layer], g_final, last)
        xs = _post(xs, tok_s, mo_s, w_out, g_mlp[layer], w_up_b[layer], w_down_b[layer], g_final, last)
        if layer == N_A_LAYERS - 1:
            k_p, v_p, kaug, vb_p, kmean = _kv_prompt(xp, g_kv, w_kv_b, k_aux)
            km3 = _gate_means(kmean, batch)
            k_s, v_s = _kv_sample(xs, g_kv, w_kv_b)
            k_s3 = k_s.reshape(n_seq, dec_seq * N_HEADS, HEAD_DIM)
            v_s3 = v_s.reshape(n_seq, dec_seq * N_HEADS, HEAD_DIM)

    mem_shape = (DEPTH, batch, N_MEM, MEM_HEADS, MEM_HEAD_DIM)
    return (xp.reshape(batch, seq, D_MODEL),
            xs.reshape(n_seq, dec_seq, D_MODEL),
            jnp.stack(conv_p),
            jnp.stack(conv_s),
            k_p.reshape(batch, seq, N_HEADS, HEAD_DIM),
            v_p.reshape(batch, seq, N_HEADS, HEAD_DIM),
            k_s.reshape(n_seq, dec_seq, N_HEADS, HEAD_DIM),
            v_s.reshape(n_seq, dec_seq, N_HEADS, HEAD_DIM),
            mem_k.reshape(mem_shape),
            mem_v.reshape(mem_shape))
```

```python
import functools

import jax
import jax.numpy as jnp
from jax import lax
from jax.experimental import pallas as pl
from jax.experimental.pallas import tpu as pltpu

F32 = jnp.float32
BF16 = jnp.bfloat16

D_MODEL = 1024
DEPTH = 4
N_A_LAYERS = DEPTH // 2
CONV_WIDTH = 3
N_HEADS = 8
HEAD_DIM = D_MODEL // N_HEADS
MOBA_BLOCK = 256
MOBA_TOPK = 3
N_MEM = 256
MEM_HEADS = 4
MEM_HEAD_DIM = D_MODEL // MEM_HEADS
D_FF = 4 * D_MODEL
ALIBI_MAX_EXP = 8.0
EPS = 1e-5
PAGE_SIZE = 128

LANES = 128
SUBLANES = 8
VMEM_LIMIT_BYTES = 56 * 1024 * 1024

TOKEN_TILE = 512
AUG_WIDTH = 2 * HEAD_DIM
MOBA_CHUNK = 2 * MOBA_BLOCK
N_BLOCK_COLS = 40
MASK_NEG = -(2.0 ** 100)
MEM_BATCH_TILE = 4


def _compiler_params(n_grid_dims):
    return pltpu.CompilerParams(
        dimension_semantics=("arbitrary",) * n_grid_dims,
        vmem_limit_bytes=VMEM_LIMIT_BYTES)


def _resident(shape):
    zeros = (0,) * len(shape)
    return pl.BlockSpec(shape, lambda *_: zeros, pipeline_mode=pl.Buffered(1))


def _rms(x, g):
    return x * lax.rsqrt(jnp.mean(x * x, axis=-1, keepdims=True) + EPS) * g


def _dot(a, b):
    return jnp.dot(a, b, preferred_element_type=F32)


def _dot_nt(a, b):
    return lax.dot_general(a, b, (((1,), (1,)), ((), ())), preferred_element_type=F32)


def _softmax_pv(s, v):
    m = jnp.max(s, axis=-1, keepdims=True)
    p = jnp.exp(s - m)
    l = jnp.sum(p, axis=-1, keepdims=True)
    return _dot(p.astype(BF16), v) / l


def _mem_kv_kernel(mem_ref, g_ref, w_ref, k_ref, v_ref, kb_ref, vb_ref):
    hn = _rms(mem_ref[...], g_ref[...]).astype(BF16)
    kv = _dot(hn, w_ref[...])
    k = kv[:, :D_MODEL]
    v = kv[:, D_MODEL:]
    k_ref[...] = k
    v_ref[...] = v
    kb_ref[...] = k.astype(BF16)
    vb_ref[...] = v.astype(BF16)


def _mem_kv(mem2d, g_mem, w_mem_kv_b):
    rows = mem2d.shape[0]
    out_f = jax.ShapeDtypeStruct((DEPTH, rows, D_MODEL), F32)
    out_b = jax.ShapeDtypeStruct((DEPTH, rows, D_MODEL), BF16)
    blk = pl.BlockSpec((None, rows, D_MODEL), lambda l: (l, 0, 0))
    return pl.pallas_call(
        _mem_kv_kernel,
        grid=(DEPTH,),
        in_specs=[
            pl.BlockSpec((rows, D_MODEL), lambda l: (0, 0)),
            pl.BlockSpec((None, 1, D_MODEL), lambda l: (l, 0, 0)),
            pl.BlockSpec((None, D_MODEL, 2 * D_MODEL), lambda l: (l, 0, 0)),
        ],
        out_specs=[blk, blk, blk, blk],
        out_shape=[out_f, out_f, out_b, out_b],
        compiler_params=_compiler_params(1),
        name="mem_kv",
    )(mem2d, g_mem.reshape(DEPTH, 1, D_MODEL), w_mem_kv_b)


def _conv_gate(h, w_ref, cw_ref, fix_prev):
    cg = _dot(h, w_ref[:, D_MODEL:2 * D_MODEL])
    hv = _dot(h, w_ref[:, 2 * D_MODEL:3 * D_MODEL])
    u = cg * hv
    prev1, prev2 = fix_prev(pltpu.roll(u, 1, 0), pltpu.roll(u, 2, 0))
    cw = cw_ref[...]
    z = cw[0:1, :] * prev2 + cw[1:2, :] * prev1 + cw[2:3, :] * u
    bg = _dot(h, w_ref[:, 0:D_MODEL])
    qm = _dot(h, w_ref[:, 3 * D_MODEL:4 * D_MODEL])
    return bg * z, qm, u


def _a_front_prompt_kernel(x_ref, g_ref, w_ref, cw_ref, tok_ref, qm_ref, ulast_ref, carry_ref):
    @pl.when(pl.program_id(1) == 0)
    def _():
        carry_ref[...] = jnp.zeros_like(carry_ref)

    h = _rms(x_ref[...], g_ref[...]).astype(BF16)
    carry = carry_ref[...]

    def fix_prev(prev1, prev2):
        row = lax.broadcasted_iota(jnp.int32, prev1.shape, 0)
        c_m1 = carry[SUBLANES - 1:SUBLANES, :]
        c_m2 = carry[SUBLANES - 2:SUBLANES - 1, :]
        prev1 = jnp.where(row < 1, c_m1, prev1)
        prev2 = jnp.where(row < 1, c_m2, jnp.where(row < 2, c_m1, prev2))
        return prev1, prev2

    tok, qm, u = _conv_gate(h, w_ref, cw_ref, fix_prev)
    tok_ref[...] = tok.astype(BF16)
    qm_ref[...] = qm.astype(BF16)
    tail = u[TOKEN_TILE - SUBLANES:, :]
    carry_ref[...] = tail
    ulast_ref[...] = tail


def _a_front_prompt(x2d, batch, g, w_b, conv_w):
    rows = x2d.shape[0]
    tiles = rows // batch // TOKEN_TILE
    tile_spec = pl.BlockSpec((TOKEN_TILE, D_MODEL), lambda b, i: (b * tiles + i, 0))
    return pl.pallas_call(
        _a_front_prompt_kernel,
        grid=(batch, tiles),
        in_specs=[
            tile_spec,
            _resident((1, D_MODEL)),
            _resident((D_MODEL, 4 * D_MODEL)),
            _resident((CONV_WIDTH, D_MODEL)),
        ],
        out_specs=[
            tile_spec,
            tile_spec,
            pl.BlockSpec((SUBLANES, D_MODEL), lambda b, i: (b * tiles + i, 0)),
        ],
        out_shape=[
            jax.ShapeDtypeStruct((rows, D_MODEL), BF16),
            jax.ShapeDtypeStruct((rows, D_MODEL), BF16),
            jax.ShapeDtypeStruct((batch * tiles * SUBLANES, D_MODEL), F32),
        ],
        scratch_shapes=[pltpu.VMEM((SUBLANES, D_MODEL), F32)],
        compiler_params=_compiler_params(2),
        name="a_front_prompt",
    )(x2d, g.reshape(1, D_MODEL), w_b, conv_w)


def _a_front_sample_kernel(seq, x_ref, g_ref, w_ref, cw_ref, s1_ref, s2_ref, tok_ref, qm_ref, u_ref):
    h = _rms(x_ref[...], g_ref[...]).astype(BF16)

    def fix_prev(prev1, prev2):
        pos = lax.broadcasted_iota(jnp.int32, prev1.shape, 0) % seq
        return (jnp.where(pos >= 1, prev1, s1_ref[...]),
                jnp.where(pos >= 2, prev2, s2_ref[...]))

    tok, qm, u = _conv_gate(h, w_ref, cw_ref, fix_prev)
    tok_ref[...] = tok.astype(BF16)
    qm_ref[...] = qm
    u_ref[...] = u


def _a_front_sample(x2d, seq, g, w_b, conv_w, state):
    rows = x2d.shape[0]
    n_seq = rows // seq
    assert seq >= CONV_WIDTH - 1
    zeros = jnp.zeros((n_seq, seq - 1, D_MODEL), F32)
    s1 = jnp.concatenate([state[:, 1:2], zeros], axis=1).reshape(rows, D_MODEL)
    s2 = jnp.concatenate([state, zeros[:, 1:]], axis=1).reshape(rows, D_MODEL)
    full = pl.BlockSpec((rows, D_MODEL), lambda i: (0, 0))
    return pl.pallas_call(
        functools.partial(_a_front_sample_kernel, seq),
        grid=(1,),
        in_specs=[full, _resident((1, D_MODEL)), _resident((D_MODEL, 4 * D_MODEL)),
                  _resident((CONV_WIDTH, D_MODEL)), full, full],
        out_specs=[full, full, full],
        out_shape=[
            jax.ShapeDtypeStruct((rows, D_MODEL), BF16),
            jax.ShapeDtypeStruct((rows, D_MODEL), F32),
            jax.ShapeDtypeStruct((rows, D_MODEL), F32),
        ],
        compiler_params=_compiler_params(1),
        name="a_front_sample",
    )(x2d, g.reshape(1, D_MODEL), w_b, conv_w, s1, s2)


def _mem_attn_prompt_kernel(q_ref, k_ref, v_ref, o_ref):
    scale = MEM_HEAD_DIM ** -0.5
    for hd in range(MEM_HEADS):
        sl = slice(hd * MEM_HEAD_DIM, (hd + 1) * MEM_HEAD_DIM)
        s = _dot_nt(q_ref[:, sl], k_ref[:, sl]) * scale
        o_ref[:, sl] = _softmax_pv(s, v_ref[:, sl]).astype(o_ref.dtype)


def _mem_attn_prompt(qm, batch, k_b, v_b):
    rows = qm.shape[0]
    tiles = rows // batch // TOKEN_TILE
    tile_spec = pl.BlockSpec((TOKEN_TILE, D_MODEL), lambda b, i: (b * tiles + i, 0))
    kv_spec = pl.BlockSpec((N_MEM, D_MODEL), lambda b, i: (b, 0))
    return pl.pallas_call(
        _mem_attn_prompt_kernel,
        grid=(batch, tiles),
        in_specs=[tile_spec, kv_spec, kv_spec],
        out_specs=tile_spec,
        out_shape=jax.ShapeDtypeStruct((rows, D_MODEL), BF16),
        compiler_params=_compiler_params(2),
        name="mem_attn_prompt",
    )(qm, k_b, v_b)


def _mem_attn_sample_kernel(q_ref, k_ref, v_ref, o_ref):
    scale = MEM_HEAD_DIM ** -0.5
    rows = q_ref.shape[1]
    cols = N_MEM * MEM_HEADS
    same_head = ((lax.broadcasted_iota(jnp.int32, (rows, cols), 0) % MEM_HEADS)
                 == (lax.broadcasted_iota(jnp.int32, (rows, cols), 1) % MEM_HEADS))
    for bb in range(MEM_BATCH_TILE):
        k = k_ref[bb].reshape(cols, MEM_HEAD_DIM).astype(BF16)
        v = v_ref[bb].reshape(cols, MEM_HEAD_DIM).astype(BF16)
        s = jnp.where(same_head, _dot_nt(q_ref[bb].astype(BF16), k) * scale, -jnp.inf)
        o_ref[bb] = _softmax_pv(s, v)


def _mem_attn_sample(qm3, layer, cache_k, cache_v):
    n_seq, rows, _ = qm3.shape
    q_spec = pl.BlockSpec((MEM_BATCH_TILE, rows, MEM_HEAD_DIM), lambda i: (i, 0, 0))
    kv_spec = pl.BlockSpec((None, MEM_BATCH_TILE, N_MEM, MEM_HEADS, MEM_HEAD_DIM),
                           lambda i: (layer, i, 0, 0, 0))
    return pl.pallas_call(
        _mem_attn_sample_kernel,
        grid=(n_seq // MEM_BATCH_TILE,),
        in_specs=[q_spec, kv_spec, kv_spec],
        out_specs=q_spec,
        out_shape=jax.ShapeDtypeStruct((n_seq, rows, MEM_HEAD_DIM), F32),
        compiler_params=_compiler_params(1),
        name="mem_attn_sample",
    )(qm3, cache_k, cache_v)


def _post_kernel(final_norm, x_ref, tok_ref, mo_ref, wo_ref, g_ref, wu_ref, wd_ref, gf_ref, o_ref):
    o_ref[...] = (x_ref[...]
                  + _dot(tok_ref[...].astype(BF16), wo_ref[0:D_MODEL, :])
                  + _dot(mo_ref[...].astype(BF16), wo_ref[D_MODEL:2 * D_MODEL, :]))
    x = o_ref[...]
    h = _rms(x, g_ref[...]).astype(BF16)
    for c in range(D_FF // D_MODEL):
        sl = slice(c * D_MODEL, (c + 1) * D_MODEL)
        a = jnp.square(jnp.maximum(_dot(h, wu_ref[:, sl]), 0.0)).astype(BF16)
        x = x + _dot(a, wd_ref[sl, :])
    o_ref[...] = x
    if final_norm:
        o_ref[...] = _rms(o_ref[...], gf_ref[...])


def _post(x2d, tok, mo, wo_b, g, wu_b, wd_b, g_final, final_norm):
    rows = x2d.shape[0]
    tile_spec = pl.BlockSpec((TOKEN_TILE, D_MODEL), lambda i: (i, 0))
    return pl.pallas_call(
        functools.partial(_post_kernel, final_norm),
        grid=(rows // TOKEN_TILE,),
        in_specs=[
            tile_spec, tile_spec, tile_spec,
            _resident((2 * D_MODEL, D_MODEL)),
            _resident((1, D_MODEL)),
            _resident((D_MODEL, D_FF)),
            _resident((D_FF, D_MODEL)),
            _resident((1, D_MODEL)),
        ],
        out_specs=tile_spec,
        out_shape=jax.ShapeDtypeStruct((rows, D_MODEL), F32),
        compiler_params=_compiler_params(1),
        name="post_final" if final_norm else "post",
    )(x2d, tok, mo, wo_b, g.reshape(1, D_MODEL), wu_b, wd_b, g_final.reshape(1, D_MODEL))


def _kv_prompt_kernel(x_ref, g_ref, w_ref, kaux_ref, k_ref, v_ref, kaug_ref, vb_ref, kmean_ref):
    h = _rms(x_ref[...], g_ref[...]).astype(BF16)
    kv = _dot(h, w_ref[...])
    k = kv[:, :D_MODEL]
    v = kv[:, D_MODEL:]
    k_ref[...] = k
    v_ref[...] = v
    vb_ref[...] = v.astype(BF16)
    kmean_ref[...] = jnp.sum(k, axis=0, keepdims=True) * (1.0 / MOBA_BLOCK)
    for hd in range(N_HEADS):
        src = slice(hd * HEAD_DIM, (hd + 1) * HEAD_DIM)
        kaug_ref[:, hd * AUG_WIDTH:hd * AUG_WIDTH + HEAD_DIM] = k[:, src].astype(BF16)
        kaug_ref[:, hd * AUG_WIDTH + HEAD_DIM:(hd + 1) * AUG_WIDTH] = kaux_ref[:, src]


def _kv_prompt(x2d, g, w_b, kaux):
    rows = x2d.shape[0]
    n_blocks = rows // MOBA_BLOCK
    seq_blocks = kaux.shape[0] // MOBA_BLOCK
    tile = lambda width: pl.BlockSpec((MOBA_BLOCK, width), lambda i: (i, 0))
    return pl.pallas_call(
        _kv_prompt_kernel,
        grid=(n_blocks,),
        in_specs=[
            tile(D_MODEL),
            _resident((1, D_MODEL)),
            _resident((D_MODEL, 2 * D_MODEL)),
            pl.BlockSpec((MOBA_BLOCK, D_MODEL), lambda i: (i % seq_blocks, 0)),
        ],
        out_specs=[
            tile(D_MODEL), tile(D_MODEL), tile(N_HEADS * AUG_WIDTH), tile(D_MODEL),
            pl.BlockSpec((None, 1, D_MODEL), lambda i: (i, 0, 0)),
        ],
        out_shape=[
            jax.ShapeDtypeStruct((rows, D_MODEL), F32),
            jax.ShapeDtypeStruct((rows, D_MODEL), F32),
            jax.ShapeDtypeStruct((rows, N_HEADS * AUG_WIDTH), BF16),
            jax.ShapeDtypeStruct((rows, D_MODEL), BF16),
            jax.ShapeDtypeStruct((n_blocks, 1, D_MODEL), F32),
        ],
        compiler_params=_compiler_params(1),
        name="kv_prompt",
    )(x2d, g.reshape(1, D_MODEL), w_b, kaux)


def _kv_sample_kernel(x_ref, g_ref, w_ref, k_ref, v_ref):
    h = _rms(x_ref[...], g_ref[...]).astype(BF16)
    kv = _dot(h, w_ref[...])
    k_ref[...] = kv[:, :D_MODEL]
    v_ref[...] = kv[:, D_MODEL:]


def _kv_sample(x2d, g, w_b):
    rows = x2d.shape[0]
    tile_spec = pl.BlockSpec((TOKEN_TILE, D_MODEL), lambda i: (i, 0))
    out = jax.ShapeDtypeStruct((rows, D_MODEL), F32)
    return pl.pallas_call(
        _kv_sample_kernel,
        grid=(rows // TOKEN_TILE,),
        in_specs=[tile_spec, _resident((1, D_MODEL)), _resident((D_MODEL, 2 * D_MODEL))],
        out_specs=[tile_spec, tile_spec],
        out_shape=[out, out],
        compiler_params=_compiler_params(1),
        name="kv_sample",
    )(x2d, g.reshape(1, D_MODEL), w_b)


def _b_front_prompt_kernel(tiles, x_ref, g_ref, w_ref, km_ref, qaux_ref, qaug_ref, qm_ref):
    h = _rms(x_ref[...], g_ref[...]).astype(BF16)
    qm_ref[...] = _dot(h, w_ref[:, D_MODEL:]).astype(BF16)
    q = _dot(h, w_ref[:, :D_MODEL])
    scale = HEAD_DIM ** -0.5
    t0 = (pl.program_id(0) % tiles) * TOKEN_TILE
    shape = (LANES, TOKEN_TILE)
    blk = lax.broadcasted_iota(jnp.int32, shape, 0)
    own = (t0 + lax.broadcasted_iota(jnp.int32, shape, 1)) // MOBA_BLOCK
    eligible = blk < own
    for hd in range(N_HEADS):
        qh = q[:, hd * HEAD_DIM:(hd + 1) * HEAD_DIM]
        q_hi = qh.astype(BF16)
        q_lo = (qh - q_hi.astype(F32)).astype(BF16)
        gate = _dot_nt(km_ref[hd], jnp.concatenate([q_hi, q_lo, q_hi], axis=1))
        gate = jnp.where(eligible, gate, -jnp.inf)
        masked = eligible
        for _ in range(MOBA_TOPK):
            best = jnp.max(gate, axis=0, keepdims=True)
            is_best = (gate == best) & (best > -jnp.inf)
            first = jnp.min(jnp.where(is_best, blk, LANES), axis=0, keepdims=True)
            pick = blk == first
            masked = masked & jnp.logical_not(pick)
            gate = jnp.where(pick, -jnp.inf, gate)
        neg = jnp.where(masked, MASK_NEG, 0.0)
        aux = neg.T + qaux_ref[:, hd * HEAD_DIM:(hd + 1) * HEAD_DIM]
        qaug_ref[:, hd * AUG_WIDTH:hd * AUG_WIDTH + HEAD_DIM] = (qh * scale).astype(BF16)
        qaug_ref[:, hd * AUG_WIDTH + HEAD_DIM:(hd + 1) * AUG_WIDTH] = aux.astype(BF16)


def _b_front_prompt(x2d, batch, g, w_b, km3, qaux):
    rows = x2d.shape[0]
    tiles = rows // batch // TOKEN_TILE
    tile = lambda width: pl.BlockSpec((TOKEN_TILE, width), lambda i: (i, 0))
    return pl.pallas_call(
        functools.partial(_b_front_prompt_kernel, tiles),
        grid=(batch * tiles,),
        in_specs=[
            tile(D_MODEL),
            _resident((1, D_MODEL)),
            _resident((D_MODEL, 2 * D_MODEL)),
            pl.BlockSpec((None, N_HEADS, LANES, 3 * HEAD_DIM), lambda i: (i // tiles, 0, 0, 0)),
            pl.BlockSpec((TOKEN_TILE, D_MODEL), lambda i: (i % tiles, 0)),
        ],
        out_specs=[tile(N_HEADS * AUG_WIDTH), tile(D_MODEL)],
        out_shape=[
            jax.ShapeDtypeStruct((rows, N_HEADS * AUG_WIDTH), BF16),
            jax.ShapeDtypeStruct((rows, D_MODEL), BF16),
        ],
        compiler_params=_compiler_params(1),
        name="b_front_prompt",
    )(x2d, g.reshape(1, D_MODEL), w_b, km3, qaux)


def _b_front_sample_kernel(x_ref, g_ref, w_ref, q_ref, qm_ref):
    h = _rms(x_ref[...], g_ref[...]).astype(BF16)
    q_ref[...] = _dot(h, w_ref[:, :D_MODEL])
    qm_ref[...] = _dot(h, w_ref[:, D_MODEL:])


def _b_front_sample(x2d, g, w_b):
    rows = x2d.shape[0]
    tile_spec = pl.BlockSpec((TOKEN_TILE, D_MODEL), lambda i: (i, 0))
    return pl.pallas_call(
        _b_front_sample_kernel,
        grid=(rows // TOKEN_TILE,),
        in_specs=[tile_spec, _resident((1, D_MODEL)), _resident((D_MODEL, 2 * D_MODEL))],
        out_specs=[tile_spec, tile_spec],
        out_shape=[jax.ShapeDtypeStruct((rows, D_MODEL), F32),
                   jax.ShapeDtypeStruct((rows, D_MODEL), F32)],
        compiler_params=_compiler_params(1),
        name="b_front_sample",
    )(x2d, g.reshape(1, D_MODEL), w_b)


def _moba_prompt_kernel(q_ref, k_ref, v_ref, o_ref):
    j = pl.program_id(2)
    q = q_ref[...]

    def scores(c):
        start = pl.multiple_of(c * MOBA_CHUNK, MOBA_CHUNK)
        return _dot_nt(q, k_ref[pl.ds(start, MOBA_CHUNK), :]), start

    def update(state, s, start):
        m, l, acc = state
        m_new = jnp.maximum(m, jnp.max(s, axis=-1, keepdims=True))
        alpha = jnp.exp(m - m_new)
        p = jnp.exp(s - m_new)
        l = alpha * l + jnp.sum(p, axis=-1, keepdims=True)
        acc = alpha * acc + _dot(p.astype(BF16), v_ref[pl.ds(start, MOBA_CHUNK), :])
        return m_new, l, acc

    s, start = scores(j)
    row = lax.broadcasted_iota(jnp.int32, s.shape, 0)
    col = lax.broadcasted_iota(jnp.int32, s.shape, 1)
    s = jnp.where(col <= row, s, MASK_NEG)
    m = jnp.max(s, axis=-1, keepdims=True)
    p = jnp.exp(s - m)
    state_a = (m, jnp.sum(p, axis=-1, keepdims=True),
               _dot(p.astype(BF16), v_ref[pl.ds(start, MOBA_CHUNK), :]))
    state_b = (jnp.full_like(m, MASK_NEG), jnp.zeros_like(m), jnp.zeros_like(state_a[2]))

    def pair(t, carry):
        sa, start_a = scores(2 * t)
        sb, start_b = scores(2 * t + 1)
        return update(carry[0], sa, start_a), update(carry[1], sb, start_b)

    state_a, state_b = lax.fori_loop(0, lax.shift_right_logical(j, 1), pair, (state_a, state_b))
    state_a = lax.cond((j & 1) == 1, lambda st: update(st, *scores(j - 1)), lambda st: st, state_a)

    m = jnp.maximum(state_a[0], state_b[0])
    wa = jnp.exp(state_a[0] - m)
    wb = jnp.exp(state_b[0] - m)
    l = wa * state_a[1] + wb * state_b[1]
    o_ref[...] = ((wa * state_a[2] + wb * state_b[2]) / l).astype(o_ref.dtype)


def _moba_prompt(qaug, kaug, vb, batch):
    rows = qaug.shape[0]
    seq = rows // batch
    assert seq % MOBA_CHUNK == 0 and MOBA_CHUNK % MOBA_BLOCK == 0
    q_tiles = seq // MOBA_CHUNK
    return pl.pallas_call(
        _moba_prompt_kernel,
        grid=(batch, N_HEADS, q_tiles),
        in_specs=[
            pl.BlockSpec((MOBA_CHUNK, AUG_WIDTH), lambda b, h, j: (b * q_tiles + j, h)),
            pl.BlockSpec((seq, AUG_WIDTH), lambda b, h, j: (b, h)),
            pl.BlockSpec((seq, HEAD_DIM), lambda b, h, j: (b, h)),
        ],
        out_specs=pl.BlockSpec((MOBA_CHUNK, HEAD_DIM), lambda b, h, j: (b * q_tiles + j, h)),
        out_shape=jax.ShapeDtypeStruct((rows, D_MODEL), BF16),
        compiler_params=_compiler_params(3),
        name="moba_prompt",
    )(qaug, kaug, vb)


def _moba_sample_kernel(seq, past_len, n_pages, pt_ref, q_ref, kn_ref, vn_ref, slope_ref, *refs):
    del pt_ref
    k_pages = refs[:n_pages]
    v_pages = refs[n_pages:2 * n_pages]
    o_ref = refs[2 * n_pages]
    n_past = n_pages // 2
    rows = seq * N_HEADS
    cols = PAGE_SIZE * N_HEADS
    scale = HEAD_DIM ** -0.5

    q = q_ref[...]
    qb = q.astype(BF16)
    zq = jnp.zeros_like(qb)
    q_pair = jnp.concatenate([jnp.concatenate([qb, zq], axis=1),
                              jnp.concatenate([zq, qb], axis=1)], axis=0)

    def both(x):
        return jnp.concatenate([x, x], axis=0)

    slope = slope_ref[...]
    t_pos = (past_len + lax.broadcasted_iota(jnp.int32, (rows, 1), 0) // N_HEADS).astype(F32)
    r_id = lax.broadcasted_iota(jnp.int32, (2 * rows, cols), 0)
    c_id = lax.broadcasted_iota(jnp.int32, (2 * rows, cols), 1)
    same_head = (r_id % N_HEADS) == (c_id % N_HEADS)
    key_pos = ((r_id // rows) * PAGE_SIZE + c_id // N_HEADS).astype(F32)
    bias0 = jnp.where(same_head, -both(slope) * (both(t_pos) - key_pos), -jnp.inf)

    gates, scores = [], []
    for n in range(n_past):
        k0 = k_pages[2 * n][...]
        k1 = k_pages[2 * n + 1][...]
        k_mean = (jnp.sum(k0, axis=0) + jnp.sum(k1, axis=0)) * (1.0 / MOBA_BLOCK)
        gates.append(jnp.sum(q * jnp.concatenate([k_mean] * seq, axis=0), axis=1, keepdims=True))
        k_pair = jnp.concatenate([k0.reshape(cols, HEAD_DIM).astype(BF16),
                                  k1.reshape(cols, HEAD_DIM).astype(BF16)], axis=1)
        scores.append(_dot_nt(q_pair, k_pair) * scale + (bias0 + both(slope) * float(n * MOBA_BLOCK)))

    selected = []
    for n in range(n_past):
        rank = jnp.zeros((rows, 1), jnp.int32)
        for mth in range(n_past):
            if mth == n:
                continue
            ahead = (gates[mth] >= gates[n]) if mth < n else (gates[mth] > gates[n])
            rank = rank + ahead.astype(jnp.int32)
        selected.append(rank < MOBA_TOPK)

    r_own = lax.broadcasted_iota(jnp.int32, (rows, rows), 0)
    c_own = lax.broadcasted_iota(jnp.int32, (rows, rows), 1)
    own_ok = ((r_own % N_HEADS) == (c_own % N_HEADS)) & (c_own // N_HEADS <= r_own // N_HEADS)
    own_bias = -slope * (t_pos - (past_len + c_own // N_HEADS).astype(F32))
    s_own = jnp.where(own_ok, _dot_nt(qb, kn_ref[...].astype(BF16)) * scale + own_bias, -jnp.inf)

    m = jnp.max(s_own, axis=1, keepdims=True)
    for n in range(n_past):
        bm = jnp.max(scores[n], axis=1, keepdims=True)
        m = jnp.maximum(m, jnp.where(selected[n], jnp.maximum(bm[:rows], bm[rows:]), -jnp.inf))

    p_own = jnp.exp(s_own - m)
    l = jnp.sum(p_own, axis=1, keepdims=True)
    acc = _dot(p_own.astype(BF16), vn_ref[...].astype(BF16))
    for n in range(n_past):
        p = jnp.where(both(selected[n]), jnp.exp(scores[n] - both(m)), 0.0)
        ps = jnp.sum(p, axis=1, keepdims=True)
        l = l + ps[:rows] + ps[rows:]
        v_pair = jnp.concatenate([v_pages[2 * n][...].reshape(cols, HEAD_DIM).astype(BF16),
                                  v_pages[2 * n + 1][...].reshape(cols, HEAD_DIM).astype(BF16)], axis=1)
        pv = _dot(p.astype(BF16), v_pair)
        acc = acc + pv[:rows, :HEAD_DIM] + pv[rows:, HEAD_DIM:]
    o_ref[...] = acc / l


def _moba_sample(page_table, q3, k_new3, v_new3, cache_k, cache_v):
    n_seq, rows, _ = q3.shape
    seq = rows // N_HEADS
    n_pages = page_table.shape[1]
    past_len = n_pages * PAGE_SIZE
    assert MOBA_BLOCK == 2 * PAGE_SIZE and past_len % MOBA_BLOCK == 0 and seq <= MOBA_BLOCK
    assert past_len // MOBA_BLOCK >= MOBA_TOPK
    slopes = jnp.exp2(-ALIBI_MAX_EXP * jnp.arange(1, N_HEADS + 1, dtype=F32) / N_HEADS)
    slope_col = jnp.tile(slopes, seq).reshape(rows, 1)
    seq_spec = pl.BlockSpec((None, rows, HEAD_DIM), lambda b, pt: (b, 0, 0))

    def page_spec(p):
        return pl.BlockSpec((None, PAGE_SIZE, N_HEADS, HEAD_DIM), lambda b, pt: (pt[b, p], 0, 0, 0))

    grid_spec = pltpu.PrefetchScalarGridSpec(
        num_scalar_prefetch=1,
        grid=(n_seq,),
        in_specs=([seq_spec, seq_spec, seq_spec, pl.BlockSpec((rows, 1), lambda b, pt: (0, 0))]
                  + [page_spec(p) for p in range(n_pages)]
                  + [page_spec(p) for p in range(n_pages)]),
        out_specs=seq_spec,
    )
    return pl.pallas_call(
        functools.partial(_moba_sample_kernel, seq, past_len, n_pages),
        grid_spec=grid_spec,
        out_shape=jax.ShapeDtypeStruct((n_seq, rows, HEAD_DIM), F32),
        compiler_params=_compiler_params(1),
        name="moba_sample",
    )(page_table, q3, k_new3, v_new3, slope_col,
      *([cache_k] * n_pages), *([cache_v] * n_pages))


def _bf16_floor(v):
    bits = lax.bitcast_convert_type(v, jnp.uint32) & jnp.uint32(0xFFFF0000)
    return lax.bitcast_convert_type(bits, F32)


def _split3(v):
    hi = _bf16_floor(v)
    r1 = v - hi
    mid = _bf16_floor(r1)
    return [hi, mid, r1 - mid]


def _moba_aux(seq):
    assert seq // MOBA_BLOCK <= N_BLOCK_COLS and N_BLOCK_COLS + 6 <= HEAD_DIM
    slopes = jnp.exp2(-ALIBI_MAX_EXP * jnp.arange(1, N_HEADS + 1, dtype=F32) / N_HEADS)
    pos = jnp.arange(seq, dtype=F32)
    sp = slopes[None, :] * pos[:, None]
    ones = jnp.ones((seq, N_HEADS), F32)
    onehot = (jnp.arange(seq)[:, None] // MOBA_BLOCK == jnp.arange(N_BLOCK_COLS)[None, :]).astype(F32)
    onehot = jnp.broadcast_to(onehot[:, None, :], (seq, N_HEADS, N_BLOCK_COLS))
    pad = jnp.zeros((seq, N_HEADS, HEAD_DIM - N_BLOCK_COLS - 6), F32)
    k_cols = jnp.stack(_split3(sp) + [ones] * 3, axis=-1)
    q_cols = jnp.stack([ones] * 3 + _split3(-sp), axis=-1)
    k_aux = jnp.concatenate([onehot, k_cols, pad], axis=-1).reshape(seq, D_MODEL).astype(BF16)
    q_aux = jnp.concatenate([jnp.zeros_like(onehot), q_cols, pad], axis=-1).reshape(seq, D_MODEL)
    return k_aux, q_aux


def _gate_means(kmean, batch):
    blocks = kmean.shape[0] // batch
    km = kmean.reshape(batch, blocks, N_HEADS, HEAD_DIM).transpose(0, 2, 1, 3)
    km = jnp.pad(km, ((0, 0), (0, 0), (0, LANES - blocks), (0, 0)))
    hi = _bf16_floor(km)
    lo = km - hi
    return jnp.concatenate([hi, hi, lo], axis=-1).astype(BF16)


def kernel(x_prompt, x_sample, state_conv, cache_k, cache_v, cache_mem_k, cache_mem_v, page_table,
           mem_prompt, g_mix, w_in_a, conv_w, w_out_a, g_kv, w_kv, w_in_b, w_out_b,
           g_mem, w_mem_kv, g_mlp, w_up, w_down, g_final):
    batch, seq, _ = x_prompt.shape
    n_seq, dec_seq, _ = x_sample.shape
    assert seq % TOKEN_TILE == 0 and (n_seq * dec_seq) % TOKEN_TILE == 0
    assert TOKEN_TILE % MOBA_BLOCK == 0 and n_seq % MEM_BATCH_TILE == 0

    w_in_a_b, w_out_a_b, w_kv_b = w_in_a.astype(BF16), w_out_a.astype(BF16), w_kv.astype(BF16)
    w_in_b_b, w_out_b_b = w_in_b.astype(BF16), w_out_b.astype(BF16)
    w_mem_kv_b, w_up_b, w_down_b = w_mem_kv.astype(BF16), w_up.astype(BF16), w_down.astype(BF16)

    mem_k, mem_v, mem_kb, mem_vb = _mem_kv(mem_prompt.reshape(batch * N_MEM, D_MODEL), g_mem, w_mem_kv_b)
    k_aux, q_aux = _moba_aux(seq)

    xp = x_prompt.reshape(batch * seq, D_MODEL)
    xs = x_sample.reshape(n_seq * dec_seq, D_MODEL)
    conv_p, conv_s = [], []
    for layer in range(DEPTH):
        if layer < N_A_LAYERS:
            tok_p, qm_p, ulast = _a_front_prompt(xp, batch, g_mix[layer], w_in_a_b[layer], conv_w[layer])
            conv_p.append(ulast.reshape(batch, -1, SUBLANES, D_MODEL)[:, -1, SUBLANES - (CONV_WIDTH - 1):])
            tok_s, qm_s, u_s = _a_front_sample(xs, dec_seq, g_mix[layer], w_in_a_b[layer], conv_w[layer],
                                               state_conv[layer])
            conv_s.append(u_s.reshape(n_seq, dec_seq, D_MODEL)[:, dec_seq - (CONV_WIDTH - 1):])
            w_out = w_out_a_b[layer]
        else:
            bi = layer - N_A_LAYERS
            qaug, qm_p = _b_front_prompt(xp, batch, g_mix[layer], w_in_b_b[bi], km3, q_aux)
            tok_p = _moba_prompt(qaug, kaug, vb_p, batch)
            q_s, qm_s = _b_front_sample(xs, g_mix[layer], w_in_b_b[bi])
            tok_s = _moba_sample(page_table, q_s.reshape(n_seq, dec_seq * N_HEADS, HEAD_DIM), k_s3, v_s3,
                                 cache_k, cache_v).reshape(n_seq * dec_seq, D_MODEL)
            w_out = w_out_b_b[bi]
        mo_p = _mem_attn_prompt(qm_p, batch, mem_kb[layer], mem_vb[layer])
        mo_s = _mem_attn_sample(qm_s.reshape(n_seq, dec_seq * MEM_HEADS, MEM_HEAD_DIM), layer, cache_mem_k,
                                cache_mem_v).reshape(n_seq * dec_seq, D_MODEL)
        last = layer == DEPTH - 1
        xp = _post(xp, tok_p, mo_p, w_out, g_mlp[layer], w_up_b[layer], w_down_b[layer], g_final, last)
        xs = _post(xs, tok_s, mo_s, w_out, g_mlp[layer], w_up_b[layer], w_down_b[layer], g_final, last)
        if layer == N_A_LAYERS - 1:
            k_p, v_p, kaug, vb_p, kmean = _kv_prompt(xp, g_kv, w_kv_b, k_aux)
            km3 = _gate_means(kmean, batch)
            k_s, v_s = _kv_sample(xs, g_kv, w_kv_b)
            k_s3 = k_s.reshape(n_seq, dec_seq * N_HEADS, HEAD_DIM)
            v_s3 = v_s.reshape(n_seq, dec_seq * N_HEADS, HEAD_DIM)

    mem_shape = (DEPTH, batch, N_MEM, MEM_HEADS, MEM_HEAD_DIM)
    return (xp.reshape(batch, seq, D_MODEL),
            xs.reshape(n_seq, dec_seq, D_MODEL),
            jnp.stack(conv_p),
            jnp.stack(conv_s),
            k_p.reshape(batch, seq, N_HEADS, HEAD_DIM),
            v_p.reshape(batch, seq, N_HEADS, HEAD_DIM),
            k_s.reshape(n_seq, dec_seq, N_HEADS, HEAD_DIM),
            v_s.reshape(n_seq, dec_seq, N_HEADS, HEAD_DIM),
            mem_k.reshape(mem_shape),
            mem_v.reshape(mem_shape))
```

```python
import functools

import jax
import jax.numpy as jnp
from jax import lax
from jax.experimental import pallas as pl
from jax.experimental.pallas import tpu as pltpu

F32 = jnp.float32
BF16 = jnp.bfloat16

D_MODEL = 1024
DEPTH = 4
N_A_LAYERS = DEPTH // 2
CONV_WIDTH = 3
N_HEADS = 8
HEAD_DIM = D_MODEL // N_HEADS
MOBA_BLOCK = 256
MOBA_TOPK = 3
N_MEM = 256
MEM_HEADS = 4
MEM_HEAD_DIM = D_MODEL // MEM_HEADS
D_FF = 4 * D_MODEL
ALIBI_MAX_EXP = 8.0
EPS = 1e-5
PAGE_SIZE = 128

LANES = 128
SUBLANES = 8
VMEM_LIMIT_BYTES = 56 * 1024 * 1024

TOKEN_TILE = 512
AUG_WIDTH = 2 * HEAD_DIM
MOBA_CHUNK = 2 * MOBA_BLOCK
ONES_ROWS = 16
N_BLOCK_COLS = 40
MASK_NEG = -(2.0 ** 100)
MEM_BATCH_TILE = 4


def _compiler_params(n_grid_dims):
    return pltpu.CompilerParams(
        dimension_semantics=("arbitrary",) * n_grid_dims,
        vmem_limit_bytes=VMEM_LIMIT_BYTES)


def _resident(shape):
    zeros = (0,) * len(shape)
    return pl.BlockSpec(shape, lambda *_: zeros, pipeline_mode=pl.Buffered(1))


def _rms(x, g):
    return x * lax.rsqrt(jnp.mean(x * x, axis=-1, keepdims=True) + EPS) * g


def _dot(a, b):
    return jnp.dot(a, b, preferred_element_type=F32)


def _dot_nt(a, b):
    return lax.dot_general(a, b, (((1,), (1,)), ((), ())), preferred_element_type=F32)


def _reduce_rows(x, op):
    rows, cols = x.shape
    groups = 8
    partial = op(x.reshape(groups, rows // groups, cols), axis=0)
    return op(partial, axis=0, keepdims=True)


def _softmax_pv(s, v):
    m = jnp.max(s, axis=-1, keepdims=True)
    p = jnp.exp(s - m)
    l = jnp.sum(p, axis=-1, keepdims=True)
    return _dot(p.astype(BF16), v) / l


def _mem_kv_kernel(mem_ref, g_ref, w_ref, k_ref, v_ref, kb_ref, vb_ref):
    hn = _rms(mem_ref[...], g_ref[...]).astype(BF16)
    kv = _dot(hn, w_ref[...])
    k = kv[:, :D_MODEL]
    v = kv[:, D_MODEL:]
    k_ref[...] = k
    v_ref[...] = v
    kb_ref[...] = k.astype(BF16)
    vb_ref[...] = v.astype(BF16)


def _mem_kv(mem2d, g_mem, w_mem_kv_b):
    rows = mem2d.shape[0]
    out_f = jax.ShapeDtypeStruct((DEPTH, rows, D_MODEL), F32)
    out_b = jax.ShapeDtypeStruct((DEPTH, rows, D_MODEL), BF16)
    blk = pl.BlockSpec((None, rows, D_MODEL), lambda l: (l, 0, 0))
    return pl.pallas_call(
        _mem_kv_kernel,
        grid=(DEPTH,),
        in_specs=[
            pl.BlockSpec((rows, D_MODEL), lambda l: (0, 0)),
            pl.BlockSpec((None, 1, D_MODEL), lambda l: (l, 0, 0)),
            pl.BlockSpec((None, D_MODEL, 2 * D_MODEL), lambda l: (l, 0, 0)),
        ],
        out_specs=[blk, blk, blk, blk],
        out_shape=[out_f, out_f, out_b, out_b],
        compiler_params=_compiler_params(1),
        name="mem_kv",
    )(mem2d, g_mem.reshape(DEPTH, 1, D_MODEL), w_mem_kv_b)


def _conv_gate(h, w_ref, cw_ref, fix_prev):
    cg = _dot(h, w_ref[:, D_MODEL:2 * D_MODEL])
    hv = _dot(h, w_ref[:, 2 * D_MODEL:3 * D_MODEL])
    u = cg * hv
    prev1, prev2 = fix_prev(pltpu.roll(u, 1, 0), pltpu.roll(u, 2, 0))
    cw = cw_ref[...]
    z = cw[0:1, :] * prev2 + cw[1:2, :] * prev1 + cw[2:3, :] * u
    bg = _dot(h, w_ref[:, 0:D_MODEL])
    qm = _dot(h, w_ref[:, 3 * D_MODEL:4 * D_MODEL])
    return bg * z, qm, u


def _a_front_prompt_kernel(x_ref, g_ref, w_ref, cw_ref, tok_ref, qm_ref, ulast_ref, carry_ref):
    @pl.when(pl.program_id(1) == 0)
    def _():
        carry_ref[...] = jnp.zeros_like(carry_ref)

    h = _rms(x_ref[...], g_ref[...]).astype(BF16)
    carry = carry_ref[...]

    def fix_prev(prev1, prev2):
        row = lax.broadcasted_iota(jnp.int32, prev1.shape, 0)
        c_m1 = carry[SUBLANES - 1:SUBLANES, :]
        c_m2 = carry[SUBLANES - 2:SUBLANES - 1, :]
        prev1 = jnp.where(row < 1, c_m1, prev1)
        prev2 = jnp.where(row < 1, c_m2, jnp.where(row < 2, c_m1, prev2))
        return prev1, prev2

    tok, qm, u = _conv_gate(h, w_ref, cw_ref, fix_prev)
    tok_ref[...] = tok.astype(BF16)
    qm_ref[...] = qm.astype(BF16)
    tail = u[TOKEN_TILE - SUBLANES:, :]
    carry_ref[...] = tail
    ulast_ref[...] = tail


def _a_front_prompt(x2d, batch, g, w_b, conv_w):
    rows = x2d.shape[0]
    tiles = rows // batch // TOKEN_TILE
    tile_spec = pl.BlockSpec((TOKEN_TILE, D_MODEL), lambda b, i: (b * tiles + i, 0))
    return pl.pallas_call(
        _a_front_prompt_kernel,
        grid=(batch, tiles),
        in_specs=[
            tile_spec,
            _resident((1, D_MODEL)),
            _resident((D_MODEL, 4 * D_MODEL)),
            _resident((CONV_WIDTH, D_MODEL)),
        ],
        out_specs=[
            tile_spec,
            tile_spec,
            pl.BlockSpec((SUBLANES, D_MODEL), lambda b, i: (b * tiles + i, 0)),
        ],
        out_shape=[
            jax.ShapeDtypeStruct((rows, D_MODEL), BF16),
            jax.ShapeDtypeStruct((rows, D_MODEL), BF16),
            jax.ShapeDtypeStruct((batch * tiles * SUBLANES, D_MODEL), F32),
        ],
        scratch_shapes=[pltpu.VMEM((SUBLANES, D_MODEL), F32)],
        compiler_params=_compiler_params(2),
        name="a_front_prompt",
    )(x2d, g.reshape(1, D_MODEL), w_b, conv_w)


def _a_front_sample_kernel(seq, x_ref, g_ref, w_ref, cw_ref, s1_ref, s2_ref, tok_ref, qm_ref, u_ref):
    h = _rms(x_ref[...], g_ref[...]).astype(BF16)

    def fix_prev(prev1, prev2):
        pos = lax.broadcasted_iota(jnp.int32, prev1.shape, 0) % seq
        return (jnp.where(pos >= 1, prev1, s1_ref[...]),
                jnp.where(pos >= 2, prev2, s2_ref[...]))

    tok, qm, u = _conv_gate(h, w_ref, cw_ref, fix_prev)
    tok_ref[...] = tok.astype(BF16)
    qm_ref[...] = qm
    u_ref[...] = u


def _a_front_sample(x2d, seq, g, w_b, conv_w, state):
    rows = x2d.shape[0]
    n_seq = rows // seq
    assert seq >= CONV_WIDTH - 1
    zeros = jnp.zeros((n_seq, seq - 1, D_MODEL), F32)
    s1 = jnp.concatenate([state[:, 1:2], zeros], axis=1).reshape(rows, D_MODEL)
    s2 = jnp.concatenate([state, zeros[:, 1:]], axis=1).reshape(rows, D_MODEL)
    full = pl.BlockSpec((rows, D_MODEL), lambda i: (0, 0))
    return pl.pallas_call(
        functools.partial(_a_front_sample_kernel, seq),
        grid=(1,),
        in_specs=[full, _resident((1, D_MODEL)), _resident((D_MODEL, 4 * D_MODEL)),
                  _resident((CONV_WIDTH, D_MODEL)), full, full],
        out_specs=[full, full, full],
        out_shape=[
            jax.ShapeDtypeStruct((rows, D_MODEL), BF16),
            jax.ShapeDtypeStruct((rows, D_MODEL), F32),
            jax.ShapeDtypeStruct((rows, D_MODEL), F32),
        ],
        compiler_params=_compiler_params(1),
        name="a_front_sample",
    )(x2d, g.reshape(1, D_MODEL), w_b, conv_w, s1, s2)


def _mem_attn_prompt_kernel(q_ref, k_ref, v_ref, o_ref):
    scale = MEM_HEAD_DIM ** -0.5
    for hd in range(MEM_HEADS):
        sl = slice(hd * MEM_HEAD_DIM, (hd + 1) * MEM_HEAD_DIM)
        s = _dot_nt(q_ref[:, sl], k_ref[:, sl]) * scale
        o_ref[:, sl] = _softmax_pv(s, v_ref[:, sl]).astype(o_ref.dtype)


def _mem_attn_prompt(qm, batch, k_b, v_b):
    rows = qm.shape[0]
    tiles = rows // batch // TOKEN_TILE
    tile_spec = pl.BlockSpec((TOKEN_TILE, D_MODEL), lambda b, i: (b * tiles + i, 0))
    kv_spec = pl.BlockSpec((N_MEM, D_MODEL), lambda b, i: (b, 0))
    return pl.pallas_call(
        _mem_attn_prompt_kernel,
        grid=(batch, tiles),
        in_specs=[tile_spec, kv_spec, kv_spec],
        out_specs=tile_spec,
        out_shape=jax.ShapeDtypeStruct((rows, D_MODEL), BF16),
        compiler_params=_compiler_params(2),
        name="mem_attn_prompt",
    )(qm, k_b, v_b)


def _mem_attn_sample_kernel(q_ref, k_ref, v_ref, o_ref):
    scale = MEM_HEAD_DIM ** -0.5
    rows = q_ref.shape[1]
    cols = N_MEM * MEM_HEADS
    same_head = ((lax.broadcasted_iota(jnp.int32, (rows, cols), 0) % MEM_HEADS)
                 == (lax.broadcasted_iota(jnp.int32, (rows, cols), 1) % MEM_HEADS))
    for bb in range(MEM_BATCH_TILE):
        k = k_ref[bb].reshape(cols, MEM_HEAD_DIM).astype(BF16)
        v = v_ref[bb].reshape(cols, MEM_HEAD_DIM).astype(BF16)
        s = jnp.where(same_head, _dot_nt(q_ref[bb].astype(BF16), k) * scale, -jnp.inf)
        o_ref[bb] = _softmax_pv(s, v)


def _mem_attn_sample(qm3, layer, cache_k, cache_v):
    n_seq, rows, _ = qm3.shape
    q_spec = pl.BlockSpec((MEM_BATCH_TILE, rows, MEM_HEAD_DIM), lambda i: (i, 0, 0))
    kv_spec = pl.BlockSpec((None, MEM_BATCH_TILE, N_MEM, MEM_HEADS, MEM_HEAD_DIM),
                           lambda i: (layer, i, 0, 0, 0))
    return pl.pallas_call(
        _mem_attn_sample_kernel,
        grid=(n_seq // MEM_BATCH_TILE,),
        in_specs=[q_spec, kv_spec, kv_spec],
        out_specs=q_spec,
        out_shape=jax.ShapeDtypeStruct((n_seq, rows, MEM_HEAD_DIM), F32),
        compiler_params=_compiler_params(1),
        name="mem_attn_sample",
    )(qm3, cache_k, cache_v)


def _post_kernel(final_norm, x_ref, tok_ref, mo_ref, wo_ref, g_ref, wu_ref, wd_ref, gf_ref, o_ref):
    o_ref[...] = (x_ref[...]
                  + _dot(tok_ref[...].astype(BF16), wo_ref[0:D_MODEL, :])
                  + _dot(mo_ref[...].astype(BF16), wo_ref[D_MODEL:2 * D_MODEL, :]))
    x = o_ref[...]
    h = _rms(x, g_ref[...]).astype(BF16)
    for c in range(D_FF // D_MODEL):
        sl = slice(c * D_MODEL, (c + 1) * D_MODEL)
        a = jnp.square(jnp.maximum(_dot(h, wu_ref[:, sl]), 0.0)).astype(BF16)
        x = x + _dot(a, wd_ref[sl, :])
    o_ref[...] = x
    if final_norm:
        o_ref[...] = _rms(o_ref[...], gf_ref[...])


def _post(x2d, tok, mo, wo_b, g, wu_b, wd_b, g_final, final_norm):
    rows = x2d.shape[0]
    tile_spec = pl.BlockSpec((TOKEN_TILE, D_MODEL), lambda i: (i, 0))
    return pl.pallas_call(
        functools.partial(_post_kernel, final_norm),
        grid=(rows // TOKEN_TILE,),
        in_specs=[
            tile_spec, tile_spec, tile_spec,
            _resident((2 * D_MODEL, D_MODEL)),
            _resident((1, D_MODEL)),
            _resident((D_MODEL, D_FF)),
            _resident((D_FF, D_MODEL)),
            _resident((1, D_MODEL)),
        ],
        out_specs=tile_spec,
        out_shape=jax.ShapeDtypeStruct((rows, D_MODEL), F32),
        compiler_params=_compiler_params(1),
        name="post_final" if final_norm else "post",
    )(x2d, tok, mo, wo_b, g.reshape(1, D_MODEL), wu_b, wd_b, g_final.reshape(1, D_MODEL))


def _kv_prompt_kernel(x_ref, g_ref, w_ref, kaux_ref, k_ref, v_ref, kaug_ref, vt_ref, kmean_ref):
    h = _rms(x_ref[...], g_ref[...]).astype(BF16)
    kv = _dot(h, w_ref[...])
    k = kv[:, :D_MODEL]
    v = kv[:, D_MODEL:]
    k_ref[...] = k
    v_ref[...] = v
    vt_ref[...] = v.T.astype(BF16)
    kmean_ref[...] = jnp.sum(k, axis=0, keepdims=True) * (1.0 / MOBA_BLOCK)
    for hd in range(N_HEADS):
        src = slice(hd * HEAD_DIM, (hd + 1) * HEAD_DIM)
        kaug_ref[:, hd * AUG_WIDTH:hd * AUG_WIDTH + HEAD_DIM] = k[:, src].astype(BF16)
        kaug_ref[:, hd * AUG_WIDTH + HEAD_DIM:(hd + 1) * AUG_WIDTH] = kaux_ref[:, src]


def _kv_prompt(x2d, g, w_b, kaux):
    rows = x2d.shape[0]
    n_blocks = rows // MOBA_BLOCK
    seq_blocks = kaux.shape[0] // MOBA_BLOCK
    tile = lambda width: pl.BlockSpec((MOBA_BLOCK, width), lambda i: (i, 0))
    return pl.pallas_call(
        _kv_prompt_kernel,
        grid=(n_blocks,),
        in_specs=[
            tile(D_MODEL),
            _resident((1, D_MODEL)),
            _resident((D_MODEL, 2 * D_MODEL)),
            pl.BlockSpec((MOBA_BLOCK, D_MODEL), lambda i: (i % seq_blocks, 0)),
        ],
        out_specs=[
            tile(D_MODEL), tile(D_MODEL), tile(N_HEADS * AUG_WIDTH),
            pl.BlockSpec((None, D_MODEL, MOBA_BLOCK), lambda i: (i, 0, 0)),
            pl.BlockSpec((None, 1, D_MODEL), lambda i: (i, 0, 0)),
        ],
        out_shape=[
            jax.ShapeDtypeStruct((rows, D_MODEL), F32),
            jax.ShapeDtypeStruct((rows, D_MODEL), F32),
            jax.ShapeDtypeStruct((rows, N_HEADS * AUG_WIDTH), BF16),
            jax.ShapeDtypeStruct((n_blocks, D_MODEL, MOBA_BLOCK), BF16),
            jax.ShapeDtypeStruct((n_blocks, 1, D_MODEL), F32),
        ],
        compiler_params=_compiler_params(1),
        name="kv_prompt",
    )(x2d, g.reshape(1, D_MODEL), w_b, kaux)


def _kv_sample_kernel(x_ref, g_ref, w_ref, k_ref, v_ref):
    h = _rms(x_ref[...], g_ref[...]).astype(BF16)
    kv = _dot(h, w_ref[...])
    k_ref[...] = kv[:, :D_MODEL]
    v_ref[...] = kv[:, D_MODEL:]


def _kv_sample(x2d, g, w_b):
    rows = x2d.shape[0]
    tile_spec = pl.BlockSpec((TOKEN_TILE, D_MODEL), lambda i: (i, 0))
    out = jax.ShapeDtypeStruct((rows, D_MODEL), F32)
    return pl.pallas_call(
        _kv_sample_kernel,
        grid=(rows // TOKEN_TILE,),
        in_specs=[tile_spec, _resident((1, D_MODEL)), _resident((D_MODEL, 2 * D_MODEL))],
        out_specs=[tile_spec, tile_spec],
        out_shape=[out, out],
        compiler_params=_compiler_params(1),
        name="kv_sample",
    )(x2d, g.reshape(1, D_MODEL), w_b)


def _b_front_prompt_kernel(x_ref, g_ref, wqt_ref, wm_ref, km_ref, qaux_ref, qaug_ref, qm_ref):
    h = _rms(x_ref[...], g_ref[...]).astype(BF16)
    qm_ref[...] = _dot(h, wm_ref[...]).astype(BF16)
    qt = _dot_nt(wqt_ref[...], h)
    scale = HEAD_DIM ** -0.5
    t0 = pl.program_id(1) * TOKEN_TILE
    shape = (LANES, TOKEN_TILE)
    blk = lax.broadcasted_iota(jnp.int32, shape, 0)
    own = (t0 + lax.broadcasted_iota(jnp.int32, shape, 1)) // MOBA_BLOCK
    eligible = blk < own
    for hd in range(N_HEADS):
        qh = qt[hd * HEAD_DIM:(hd + 1) * HEAD_DIM, :]
        q_hi = qh.astype(BF16)
        q_lo = (qh - q_hi.astype(F32)).astype(BF16)
        gate = _dot(km_ref[hd], jnp.concatenate([q_hi, q_lo, q_hi], axis=0))
        gate = jnp.where(eligible, gate, -jnp.inf)
        masked = eligible
        for _ in range(MOBA_TOPK):
            best = jnp.max(gate, axis=0, keepdims=True)
            is_best = (gate == best) & (best > -jnp.inf)
            first = jnp.min(jnp.where(is_best, blk, LANES), axis=0, keepdims=True)
            pick = blk == first
            masked = masked & jnp.logical_not(pick)
            gate = jnp.where(pick, -jnp.inf, gate)
        aux = jnp.where(masked, MASK_NEG, 0.0) + qaux_ref[hd * HEAD_DIM:(hd + 1) * HEAD_DIM, :]
        qaug_ref[hd * AUG_WIDTH:hd * AUG_WIDTH + HEAD_DIM, :] = (qh * scale).astype(BF16)
        qaug_ref[hd * AUG_WIDTH + HEAD_DIM:(hd + 1) * AUG_WIDTH, :] = aux.astype(BF16)


def _b_front_prompt(x2d, batch, g, wq_t, wm, km3, qaux_t):
    rows = x2d.shape[0]
    seq = rows // batch
    tiles = seq // TOKEN_TILE
    tile_spec = pl.BlockSpec((TOKEN_TILE, D_MODEL), lambda b, i: (b * tiles + i, 0))
    return pl.pallas_call(
        _b_front_prompt_kernel,
        grid=(batch, tiles),
        in_specs=[
            tile_spec,
            _resident((1, D_MODEL)),
            _resident((D_MODEL, D_MODEL)),
            _resident((D_MODEL, D_MODEL)),
            pl.BlockSpec((None, N_HEADS, LANES, 3 * HEAD_DIM), lambda b, i: (b, 0, 0, 0)),
            pl.BlockSpec((D_MODEL, TOKEN_TILE), lambda b, i: (0, i)),
        ],
        out_specs=[pl.BlockSpec((N_HEADS * AUG_WIDTH, TOKEN_TILE), lambda b, i: (b, i)), tile_spec],
        out_shape=[
            jax.ShapeDtypeStruct((batch * N_HEADS * AUG_WIDTH, seq), BF16),
            jax.ShapeDtypeStruct((rows, D_MODEL), BF16),
        ],
        compiler_params=_compiler_params(2),
        name="b_front_prompt",
    )(x2d, g.reshape(1, D_MODEL), wq_t, wm, km3, qaux_t)


def _b_front_sample_kernel(x_ref, g_ref, w_ref, q_ref, qm_ref):
    h = _rms(x_ref[...], g_ref[...]).astype(BF16)
    q_ref[...] = _dot(h, w_ref[:, :D_MODEL])
    qm_ref[...] = _dot(h, w_ref[:, D_MODEL:])


def _b_front_sample(x2d, g, w_b):
    rows = x2d.shape[0]
    tile_spec = pl.BlockSpec((TOKEN_TILE, D_MODEL), lambda i: (i, 0))
    return pl.pallas_call(
        _b_front_sample_kernel,
        grid=(rows // TOKEN_TILE,),
        in_specs=[tile_spec, _resident((1, D_MODEL)), _resident((D_MODEL, 2 * D_MODEL))],
        out_specs=[tile_spec, tile_spec],
        out_shape=[jax.ShapeDtypeStruct((rows, D_MODEL), F32),
                   jax.ShapeDtypeStruct((rows, D_MODEL), F32)],
        compiler_params=_compiler_params(1),
        name="b_front_sample",
    )(x2d, g.reshape(1, D_MODEL), w_b)


def _moba_prompt_kernel(q_ref, k_ref, v_ref, o_ref):
    j = pl.program_id(2)
    qt = q_ref[...]
    blocks_per_chunk = MOBA_CHUNK // MOBA_BLOCK

    def scores(c):
        start = pl.multiple_of(c * MOBA_CHUNK, MOBA_CHUNK)
        return _dot(k_ref[pl.ds(start, MOBA_CHUNK), :], qt)

    ones_rows = jnp.ones((ONES_ROWS, MOBA_CHUNK), BF16)

    def values_t(c):
        parts = [v_ref[c * blocks_per_chunk + i] for i in range(blocks_per_chunk)]
        return jnp.concatenate([jnp.concatenate(parts, axis=1), ones_rows], axis=0)

    def probs(s, m):
        return jnp.exp(s - m).astype(BF16)

    def update(state, c):
        m, acc = state
        s = scores(c)
        m_new = jnp.maximum(m, _reduce_rows(s, jnp.max))
        acc = jnp.exp(m - m_new) * acc + _dot(values_t(c), probs(s, m_new))
        return m_new, acc

    s = scores(j)
    key = lax.broadcasted_iota(jnp.int32, s.shape, 0)
    query = lax.broadcasted_iota(jnp.int32, s.shape, 1)
    s = jnp.where(key <= query, s, MASK_NEG)
    m = _reduce_rows(s, jnp.max)
    state_a = (m, _dot(values_t(j), probs(s, m)))
    state_b = (jnp.full_like(m, MASK_NEG), jnp.zeros_like(state_a[1]))

    def pair(t, carry):
        (ma, acca), (mb, accb) = carry
        sa = scores(2 * t)
        ma_new = jnp.maximum(ma, _reduce_rows(sa, jnp.max))
        sb = scores(2 * t + 1)
        pa = probs(sa, ma_new)
        mb_new = jnp.maximum(mb, _reduce_rows(sb, jnp.max))
        acca = jnp.exp(ma - ma_new) * acca + _dot(values_t(2 * t), pa)
        pb = probs(sb, mb_new)
        accb = jnp.exp(mb - mb_new) * accb + _dot(values_t(2 * t + 1), pb)
        return (ma_new, acca), (mb_new, accb)

    state_a, state_b = lax.fori_loop(0, lax.shift_right_logical(j, 1), pair, (state_a, state_b))
    state_a = lax.cond((j & 1) == 1, lambda st: update(st, j - 1), lambda st: st, state_a)

    m = jnp.maximum(state_a[0], state_b[0])
    acc = jnp.exp(state_a[0] - m) * state_a[1] + jnp.exp(state_b[0] - m) * state_b[1]
    out_t = acc[:HEAD_DIM] / acc[HEAD_DIM:HEAD_DIM + 1]
    o_ref[...] = out_t.T.astype(o_ref.dtype)


def _moba_prompt(qaug_t, kaug, v_t, batch):
    rows = kaug.shape[0]
    seq = rows // batch
    assert seq % MOBA_CHUNK == 0 and MOBA_CHUNK % MOBA_BLOCK == 0
    q_tiles = seq // MOBA_CHUNK
    seq_blocks = seq // MOBA_BLOCK
    return pl.pallas_call(
        _moba_prompt_kernel,
        grid=(batch, N_HEADS, q_tiles),
        in_specs=[
            pl.BlockSpec((AUG_WIDTH, MOBA_CHUNK), lambda b, h, j: (b * N_HEADS + h, j)),
            pl.BlockSpec((seq, AUG_WIDTH), lambda b, h, j: (b, h)),
            pl.BlockSpec((seq_blocks, HEAD_DIM, MOBA_BLOCK), lambda b, h, j: (b, h, 0)),
        ],
        out_specs=pl.BlockSpec((MOBA_CHUNK, HEAD_DIM), lambda b, h, j: (b * q_tiles + j, h)),
        out_shape=jax.ShapeDtypeStruct((rows, D_MODEL), BF16),
        compiler_params=_compiler_params(3),
        name="moba_prompt",
    )(qaug_t, kaug, v_t)


def _moba_sample_kernel(seq, past_len, n_pages, pt_ref, q_ref, kn_ref, vn_ref, slope_ref, *refs):
    del pt_ref
    k_pages = refs[:n_pages]
    v_pages = refs[n_pages:2 * n_pages]
    o_ref = refs[2 * n_pages]
    n_past = n_pages // 2
    rows = seq * N_HEADS
    cols = PAGE_SIZE * N_HEADS
    scale = HEAD_DIM ** -0.5

    q = q_ref[...]
    qb = q.astype(BF16)
    zq = jnp.zeros_like(qb)
    q_pair = jnp.concatenate([jnp.concatenate([qb, zq], axis=1),
                              jnp.concatenate([zq, qb], axis=1)], axis=0)

    def both(x):
        return jnp.concatenate([x, x], axis=0)

    slope = slope_ref[...]
    t_pos = (past_len + lax.broadcasted_iota(jnp.int32, (rows, 1), 0) // N_HEADS).astype(F32)
    r_id = lax.broadcasted_iota(jnp.int32, (2 * rows, cols), 0)
    c_id = lax.broadcasted_iota(jnp.int32, (2 * rows, cols), 1)
    same_head = (r_id % N_HEADS) == (c_id % N_HEADS)
    key_pos = ((r_id // rows) * PAGE_SIZE + c_id // N_HEADS).astype(F32)
    bias0 = jnp.where(same_head, -both(slope) * (both(t_pos) - key_pos), -jnp.inf)

    gates, scores = [], []
    for n in range(n_past):
        k0 = k_pages[2 * n][...]
        k1 = k_pages[2 * n + 1][...]
        k_mean = (jnp.sum(k0, axis=0) + jnp.sum(k1, axis=0)) * (1.0 / MOBA_BLOCK)
        gates.append(jnp.sum(q * jnp.concatenate([k_mean] * seq, axis=0), axis=1, keepdims=True))
        k_pair = jnp.concatenate([k0.reshape(cols, HEAD_DIM).astype(BF16),
                                  k1.reshape(cols, HEAD_DIM).astype(BF16)], axis=1)
        scores.append(_dot_nt(q_pair, k_pair) * scale + (bias0 + both(slope) * float(n * MOBA_BLOCK)))

    selected = []
    for n in range(n_past):
        rank = jnp.zeros((rows, 1), jnp.int32)
        for mth in range(n_past):
            if mth == n:
                continue
            ahead = (gates[mth] >= gates[n]) if mth < n else (gates[mth] > gates[n])
            rank = rank + ahead.astype(jnp.int32)
        selected.append(rank < MOBA_TOPK)

    r_own = lax.broadcasted_iota(jnp.int32, (rows, rows), 0)
    c_own = lax.broadcasted_iota(jnp.int32, (rows, rows), 1)
    own_ok = ((r_own % N_HEADS) == (c_own % N_HEADS)) & (c_own // N_HEADS <= r_own // N_HEADS)
    own_bias = -slope * (t_pos - (past_len + c_own // N_HEADS).astype(F32))
    s_own = jnp.where(own_ok, _dot_nt(qb, kn_ref[...].astype(BF16)) * scale + own_bias, -jnp.inf)

    m = jnp.max(s_own, axis=1, keepdims=True)
    for n in range(n_past):
        bm = jnp.max(scores[n], axis=1, keepdims=True)
        m = jnp.maximum(m, jnp.where(selected[n], jnp.maximum(bm[:rows], bm[rows:]), -jnp.inf))

    p_own = jnp.exp(s_own - m)
    l = jnp.sum(p_own, axis=1, keepdims=True)
    acc = _dot(p_own.astype(BF16), vn_ref[...].astype(BF16))
    for n in range(n_past):
        p = jnp.where(both(selected[n]), jnp.exp(scores[n] - both(m)), 0.0)
        ps = jnp.sum(p, axis=1, keepdims=True)
        l = l + ps[:rows] + ps[rows:]
        v_pair = jnp.concatenate([v_pages[2 * n][...].reshape(cols, HEAD_DIM).astype(BF16),
                                  v_pages[2 * n + 1][...].reshape(cols, HEAD_DIM).astype(BF16)], axis=1)
        pv = _dot(p.astype(BF16), v_pair)
        acc = acc + pv[:rows, :HEAD_DIM] + pv[rows:, HEAD_DIM:]
    o_ref[...] = acc / l


def _moba_sample(page_table, q3, k_new3, v_new3, cache_k, cache_v):
    n_seq, rows, _ = q3.shape
    seq = rows // N_HEADS
    n_pages = page_table.shape[1]
    past_len = n_pages * PAGE_SIZE
    assert MOBA_BLOCK == 2 * PAGE_SIZE and past_len % MOBA_BLOCK == 0 and seq <= MOBA_BLOCK
    assert past_len // MOBA_BLOCK >= MOBA_TOPK
    slopes = jnp.exp2(-ALIBI_MAX_EXP * jnp.arange(1, N_HEADS + 1, dtype=F32) / N_HEADS)
    slope_col = jnp.tile(slopes, seq).reshape(rows, 1)
    seq_spec = pl.BlockSpec((None, rows, HEAD_DIM), lambda b, pt: (b, 0, 0))

    def page_spec(p):
        return pl.BlockSpec((None, PAGE_SIZE, N_HEADS, HEAD_DIM), lambda b, pt: (pt[b, p], 0, 0, 0))

    grid_spec = pltpu.PrefetchScalarGridSpec(
        num_scalar_prefetch=1,
        grid=(n_seq,),
        in_specs=([seq_spec, seq_spec, seq_spec, pl.BlockSpec((rows, 1), lambda b, pt: (0, 0))]
                  + [page_spec(p) for p in range(n_pages)]
                  + [page_spec(p) for p in range(n_pages)]),
        out_specs=seq_spec,
    )
    return pl.pallas_call(
        functools.partial(_moba_sample_kernel, seq, past_len, n_pages),
        grid_spec=grid_spec,
        out_shape=jax.ShapeDtypeStruct((n_seq, rows, HEAD_DIM), F32),
        compiler_params=_compiler_params(1),
        name="moba_sample",
    )(page_table, q3, k_new3, v_new3, slope_col,
      *([cache_k] * n_pages), *([cache_v] * n_pages))


def _bf16_floor(v):
    bits = lax.bitcast_convert_type(v, jnp.uint32) & jnp.uint32(0xFFFF0000)
    return lax.bitcast_convert_type(bits, F32)


def _split3(v):
    hi = _bf16_floor(v)
    r1 = v - hi
    mid = _bf16_floor(r1)
    return [hi, mid, r1 - mid]


def _moba_aux(seq):
    assert seq // MOBA_BLOCK <= N_BLOCK_COLS and N_BLOCK_COLS + 6 <= HEAD_DIM
    slopes = jnp.exp2(-ALIBI_MAX_EXP * jnp.arange(1, N_HEADS + 1, dtype=F32) / N_HEADS)
    pos = jnp.arange(seq, dtype=F32)
    sp = slopes[None, :] * pos[:, None]
    ones = jnp.ones((seq, N_HEADS), F32)
    onehot = (jnp.arange(seq)[:, None] // MOBA_BLOCK == jnp.arange(N_BLOCK_COLS)[None, :]).astype(F32)
    onehot = jnp.broadcast_to(onehot[:, None, :], (seq, N_HEADS, N_BLOCK_COLS))
    pad = jnp.zeros((seq, N_HEADS, HEAD_DIM - N_BLOCK_COLS - 6), F32)
    k_cols = jnp.stack(_split3(sp) + [ones] * 3, axis=-1)
    q_cols = jnp.stack([ones] * 3 + _split3(-sp), axis=-1)
    k_aux = jnp.concatenate([onehot, k_cols, pad], axis=-1).reshape(seq, D_MODEL).astype(BF16)
    q_aux = jnp.concatenate([jnp.zeros_like(onehot), q_cols, pad], axis=-1).reshape(seq, D_MODEL)
    return k_aux, q_aux.T


def _gate_means(kmean, batch):
    blocks = kmean.shape[0] // batch
    km = kmean.reshape(batch, blocks, N_HEADS, HEAD_DIM).transpose(0, 2, 1, 3)
    km = jnp.pad(km, ((0, 0), (0, 0), (0, LANES - blocks), (0, 0)))
    hi = _bf16_floor(km)
    lo = km - hi
    return jnp.concatenate([hi, hi, lo], axis=-1).astype(BF16)


def kernel(x_prompt, x_sample, state_conv, cache_k, cache_v, cache_mem_k, cache_mem_v, page_table,
           mem_prompt, g_mix, w_in_a, conv_w, w_out_a, g_kv, w_kv, w_in_b, w_out_b,
           g_mem, w_mem_kv, g_mlp, w_up, w_down, g_final):
    batch, seq, _ = x_prompt.shape
    n_seq, dec_seq, _ = x_sample.shape
    assert seq % TOKEN_TILE == 0 and (n_seq * dec_seq) % TOKEN_TILE == 0
    assert TOKEN_TILE % MOBA_BLOCK == 0 and n_seq % MEM_BATCH_TILE == 0

    w_in_a_b, w_out_a_b, w_kv_b = w_in_a.astype(BF16), w_out_a.astype(BF16), w_kv.astype(BF16)
    w_in_b_b, w_out_b_b = w_in_b.astype(BF16), w_out_b.astype(BF16)
    w_mem_kv_b, w_up_b, w_down_b = w_mem_kv.astype(BF16), w_up.astype(BF16), w_down.astype(BF16)

    mem_k, mem_v, mem_kb, mem_vb = _mem_kv(mem_prompt.reshape(batch * N_MEM, D_MODEL), g_mem, w_mem_kv_b)
    k_aux, q_aux_t = _moba_aux(seq)

    xp = x_prompt.reshape(batch * seq, D_MODEL)
    xs = x_sample.reshape(n_seq * dec_seq, D_MODEL)
    conv_p, conv_s = [], []
    for layer in range(DEPTH):
        if layer < N_A_LAYERS:
            tok_p, qm_p, ulast = _a_front_prompt(xp, batch, g_mix[layer], w_in_a_b[layer], conv_w[layer])
            conv_p.append(ulast.reshape(batch, -1, SUBLANES, D_MODEL)[:, -1, SUBLANES - (CONV_WIDTH - 1):])
            tok_s, qm_s, u_s = _a_front_sample(xs, dec_seq, g_mix[layer], w_in_a_b[layer], conv_w[layer],
                                               state_conv[layer])
            conv_s.append(u_s.reshape(n_seq, dec_seq, D_MODEL)[:, dec_seq - (CONV_WIDTH - 1):])
            w_out = w_out_a_b[layer]
        else:
            bi = layer - N_A_LAYERS
            qaug_t, qm_p = _b_front_prompt(xp, batch, g_mix[layer], w_in_b_b[bi, :, :D_MODEL].T,
                                           w_in_b_b[bi, :, D_MODEL:], km3, q_aux_t)
            tok_p = _moba_prompt(qaug_t, kaug, v_t, batch)
            q_s, qm_s = _b_front_sample(xs, g_mix[layer], w_in_b_b[bi])
            tok_s = _moba_sample(page_table, q_s.reshape(n_seq, dec_seq * N_HEADS, HEAD_DIM), k_s3, v_s3,
                                 cache_k, cache_v).reshape(n_seq * dec_seq, D_MODEL)
            w_out = w_out_b_b[bi]
        mo_p = _mem_attn_prompt(qm_p, batch, mem_kb[layer], mem_vb[layer])
        mo_s = _mem_attn_sample(qm_s.reshape(n_seq, dec_seq * MEM_HEADS, MEM_HEAD_DIM), layer, cache_mem_k,
                                cache_mem_v).reshape(n_seq * dec_seq, D_MODEL)
        last = layer == DEPTH - 1
        xp = _post(xp, tok_p, mo_p, w_out, g_mlp[layer], w_up_b[layer], w_down_b[layer], g_final, last)
        xs = _post(xs, tok_s, mo_s, w_out, g_mlp[layer], w_up_b[layer], w_down_b[layer], g_final, last)
        if layer == N_A_LAYERS - 1:
            k_p, v_p, kaug, v_t, kmean = _kv_prompt(xp, g_kv, w_kv_b, k_aux)
            km3 = _gate_means(kmean, batch)
            k_s, v_s = _kv_sample(xs, g_kv, w_kv_b)
            k_s3 = k_s.reshape(n_seq, dec_seq * N_HEADS, HEAD_DIM)
            v_s3 = v_s.reshape(n_seq, dec_seq * N_HEADS, HEAD_DIM)

    mem_shape = (DEPTH, batch, N_MEM, MEM_HEADS, MEM_HEAD_DIM)
    return (xp.reshape(batch, seq, D_MODEL),
            xs.reshape(n_seq, dec_seq, D_MODEL),
            jnp.stack(conv_p),
            jnp.stack(conv_s),
            k_p.reshape(batch, seq, N_HEADS, HEAD_DIM),
            v_p.reshape(batch, seq, N_HEADS, HEAD_DIM),
            k_s.reshape(n_seq, dec_seq, N_HEADS, HEAD_DIM),
            v_s.reshape(n_seq, dec_seq, N_HEADS, HEAD_DIM),
            mem_k.reshape(mem_shape),
            mem_v.reshape(mem_shape))
```

```python
import functools

import jax
import jax.numpy as jnp
from jax import lax
from jax.experimental import pallas as pl
from jax.experimental.pallas import tpu as pltpu

F32 = jnp.float32
BF16 = jnp.bfloat16

D_MODEL = 1024
DEPTH = 4
N_A_LAYERS = DEPTH // 2
CONV_WIDTH = 3
N_HEADS = 8
HEAD_DIM = D_MODEL // N_HEADS
MOBA_BLOCK = 256
MOBA_TOPK = 3
N_MEM = 256
MEM_HEADS = 4
MEM_HEAD_DIM = D_MODEL // MEM_HEADS
D_FF = 4 * D_MODEL
ALIBI_MAX_EXP = 8.0
EPS = 1e-5
PAGE_SIZE = 128

LANES = 128
SUBLANES = 8
VMEM_LIMIT_BYTES = 56 * 1024 * 1024

TOKEN_TILE = 512
AUG_WIDTH = 2 * HEAD_DIM
MOBA_CHUNK = 2 * MOBA_BLOCK
MOBA_Q_TILE = 2 * MOBA_CHUNK
ONES_ROWS = 16
N_BLOCK_COLS = 40
GATE_ROWS = 48
MASK_NEG = -(2.0 ** 100)
MEM_BATCH_TILE = 4


def _compiler_params(n_grid_dims):
    return pltpu.CompilerParams(
        dimension_semantics=("arbitrary",) * n_grid_dims,
        vmem_limit_bytes=VMEM_LIMIT_BYTES)


def _resident(shape, layer=None):
    zeros = (0,) * len(shape)
    if layer is None:
        return pl.BlockSpec(shape, lambda *_: zeros, pipeline_mode=pl.Buffered(1))
    return pl.BlockSpec((None,) + tuple(shape), lambda *_: (layer,) + zeros, pipeline_mode=pl.Buffered(1))


def _rms(x, g):
    return x * lax.rsqrt(jnp.mean(x * x, axis=-1, keepdims=True) + EPS) * g


def _dot(a, b):
    return jnp.dot(a, b, preferred_element_type=F32)


def _dot_nt(a, b):
    return lax.dot_general(a, b, (((1,), (1,)), ((), ())), preferred_element_type=F32)


def _reduce_rows(x, op):
    rows, cols = x.shape
    groups = 8
    partial = op(x.reshape(groups, rows // groups, cols), axis=0)
    return op(partial, axis=0, keepdims=True)


def _softmax_pv(s, v):
    m = jnp.max(s, axis=-1, keepdims=True)
    p = jnp.exp(s - m)
    l = jnp.sum(p, axis=-1, keepdims=True)
    return _dot(p.astype(BF16), v) / l


def _mem_kv_kernel(mem_ref, g_ref, w_ref, k_ref, v_ref, kb_ref, vb_ref):
    hn = _rms(mem_ref[...], g_ref[...]).astype(BF16)
    kv = _dot(hn, w_ref[...])
    k = kv[:, :D_MODEL]
    v = kv[:, D_MODEL:]
    k_ref[...] = k
    v_ref[...] = v
    kb_ref[...] = k.astype(BF16)
    vb_ref[...] = v.astype(BF16)


def _mem_kv(mem2d, g_mem, w_mem_kv_b):
    rows = mem2d.shape[0]
    out_f = jax.ShapeDtypeStruct((DEPTH, rows, D_MODEL), F32)
    out_b = jax.ShapeDtypeStruct((DEPTH, rows, D_MODEL), BF16)
    blk = pl.BlockSpec((None, rows, D_MODEL), lambda l: (l, 0, 0))
    return pl.pallas_call(
        _mem_kv_kernel,
        grid=(DEPTH,),
        in_specs=[
            pl.BlockSpec((rows, D_MODEL), lambda l: (0, 0)),
            pl.BlockSpec((None, 1, D_MODEL), lambda l: (l, 0, 0)),
            pl.BlockSpec((None, D_MODEL, 2 * D_MODEL), lambda l: (l, 0, 0)),
        ],
        out_specs=[blk, blk, blk, blk],
        out_shape=[out_f, out_f, out_b, out_b],
        compiler_params=_compiler_params(1),
        name="mem_kv",
    )(mem2d, g_mem.reshape(DEPTH, 1, D_MODEL), w_mem_kv_b)


def _conv_gate(h, w_ref, cw_ref, fix_prev):
    cg = _dot(h, w_ref[:, D_MODEL:2 * D_MODEL])
    hv = _dot(h, w_ref[:, 2 * D_MODEL:3 * D_MODEL])
    u = cg * hv
    prev1, prev2 = fix_prev(pltpu.roll(u, 1, 0), pltpu.roll(u, 2, 0))
    cw = cw_ref[...]
    z = cw[0:1, :] * prev2 + cw[1:2, :] * prev1 + cw[2:3, :] * u
    bg = _dot(h, w_ref[:, 0:D_MODEL])
    qm = _dot(h, w_ref[:, 3 * D_MODEL:4 * D_MODEL])
    return bg * z, qm, u


def _a_front_prompt_kernel(x_ref, g_ref, w_ref, cw_ref, tok_ref, qm_ref, ulast_ref, carry_ref):
    @pl.when(pl.program_id(1) == 0)
    def _():
        carry_ref[...] = jnp.zeros_like(carry_ref)

    h = _rms(x_ref[...], g_ref[...]).astype(BF16)
    carry = carry_ref[...]

    def fix_prev(prev1, prev2):
        row = lax.broadcasted_iota(jnp.int32, prev1.shape, 0)
        c_m1 = carry[SUBLANES - 1:SUBLANES, :]
        c_m2 = carry[SUBLANES - 2:SUBLANES - 1, :]
        prev1 = jnp.where(row < 1, c_m1, prev1)
        prev2 = jnp.where(row < 1, c_m2, jnp.where(row < 2, c_m1, prev2))
        return prev1, prev2

    tok, qm, u = _conv_gate(h, w_ref, cw_ref, fix_prev)
    tok_ref[...] = tok.astype(BF16)
    qm_ref[...] = qm.astype(BF16)
    tail = u[TOKEN_TILE - SUBLANES:, :]
    carry_ref[...] = tail
    ulast_ref[...] = tail


def _a_front_prompt(x2d, batch, g, w_b, layer, conv_w):
    rows = x2d.shape[0]
    tiles = rows // batch // TOKEN_TILE
    tile_spec = pl.BlockSpec((TOKEN_TILE, D_MODEL), lambda b, i: (b * tiles + i, 0))
    return pl.pallas_call(
        _a_front_prompt_kernel,
        grid=(batch, tiles),
        in_specs=[
            tile_spec,
            _resident((1, D_MODEL)),
            _resident((D_MODEL, 4 * D_MODEL), layer),
            _resident((CONV_WIDTH, D_MODEL)),
        ],
        out_specs=[
            tile_spec,
            tile_spec,
            pl.BlockSpec((SUBLANES, D_MODEL), lambda b, i: (b * tiles + i, 0)),
        ],
        out_shape=[
            jax.ShapeDtypeStruct((rows, D_MODEL), BF16),
            jax.ShapeDtypeStruct((rows, D_MODEL), BF16),
            jax.ShapeDtypeStruct((batch * tiles * SUBLANES, D_MODEL), F32),
        ],
        scratch_shapes=[pltpu.VMEM((SUBLANES, D_MODEL), F32)],
        compiler_params=_compiler_params(2),
        name="a_front_prompt",
    )(x2d, g.reshape(1, D_MODEL), w_b, conv_w)


def _a_front_sample_kernel(seq, x_ref, g_ref, w_ref, cw_ref, s1_ref, s2_ref, tok_ref, qm_ref, u_ref):
    h = _rms(x_ref[...], g_ref[...]).astype(BF16)

    def fix_prev(prev1, prev2):
        pos = lax.broadcasted_iota(jnp.int32, prev1.shape, 0) % seq
        return (jnp.where(pos >= 1, prev1, s1_ref[...]),
                jnp.where(pos >= 2, prev2, s2_ref[...]))

    tok, qm, u = _conv_gate(h, w_ref, cw_ref, fix_prev)
    tok_ref[...] = tok.astype(BF16)
    qm_ref[...] = qm
    u_ref[...] = u


def _a_front_sample(x2d, seq, g, w_b, layer, conv_w, state):
    rows = x2d.shape[0]
    n_seq = rows // seq
    assert seq >= CONV_WIDTH - 1
    zeros = jnp.zeros((n_seq, seq - 1, D_MODEL), F32)
    s1 = jnp.concatenate([state[:, 1:2], zeros], axis=1).reshape(rows, D_MODEL)
    s2 = jnp.concatenate([state, zeros[:, 1:]], axis=1).reshape(rows, D_MODEL)
    full = pl.BlockSpec((rows, D_MODEL), lambda i: (0, 0))
    return pl.pallas_call(
        functools.partial(_a_front_sample_kernel, seq),
        grid=(1,),
        in_specs=[full, _resident((1, D_MODEL)), _resident((D_MODEL, 4 * D_MODEL), layer),
                  _resident((CONV_WIDTH, D_MODEL)), full, full],
        out_specs=[full, full, full],
        out_shape=[
            jax.ShapeDtypeStruct((rows, D_MODEL), BF16),
            jax.ShapeDtypeStruct((rows, D_MODEL), F32),
            jax.ShapeDtypeStruct((rows, D_MODEL), F32),
        ],
        compiler_params=_compiler_params(1),
        name="a_front_sample",
    )(x2d, g.reshape(1, D_MODEL), w_b, conv_w, s1, s2)


def _mem_attn_prompt_kernel(q_ref, k_ref, v_ref, o_ref):
    scale = MEM_HEAD_DIM ** -0.5
    for hd in range(MEM_HEADS):
        sl = slice(hd * MEM_HEAD_DIM, (hd + 1) * MEM_HEAD_DIM)
        s = _dot_nt(q_ref[:, sl], k_ref[:, sl]) * scale
        o_ref[:, sl] = _softmax_pv(s, v_ref[:, sl]).astype(o_ref.dtype)


def _mem_attn_prompt(qm, batch, k_b, v_b):
    rows = qm.shape[0]
    tiles = rows // batch // TOKEN_TILE
    tile_spec = pl.BlockSpec((TOKEN_TILE, D_MODEL), lambda b, i: (b * tiles + i, 0))
    kv_spec = pl.BlockSpec((N_MEM, D_MODEL), lambda b, i: (b, 0))
    return pl.pallas_call(
        _mem_attn_prompt_kernel,
        grid=(batch, tiles),
        in_specs=[tile_spec, kv_spec, kv_spec],
        out_specs=tile_spec,
        out_shape=jax.ShapeDtypeStruct((rows, D_MODEL), BF16),
        compiler_params=_compiler_params(2),
        name="mem_attn_prompt",
    )(qm, k_b, v_b)


def _mem_attn_sample_kernel(q_ref, k_ref, v_ref, o_ref):
    scale = MEM_HEAD_DIM ** -0.5
    rows = q_ref.shape[1]
    cols = N_MEM * MEM_HEADS
    same_head = ((lax.broadcasted_iota(jnp.int32, (rows, cols), 0) % MEM_HEADS)
                 == (lax.broadcasted_iota(jnp.int32, (rows, cols), 1) % MEM_HEADS))
    for bb in range(MEM_BATCH_TILE):
        k = k_ref[bb].reshape(cols, MEM_HEAD_DIM).astype(BF16)
        v = v_ref[bb].reshape(cols, MEM_HEAD_DIM).astype(BF16)
        s = jnp.where(same_head, _dot_nt(q_ref[bb].astype(BF16), k) * scale, -jnp.inf)
        o_ref[bb] = _softmax_pv(s, v)


def _mem_attn_sample(qm3, layer, cache_k, cache_v):
    n_seq, rows, _ = qm3.shape
    q_spec = pl.BlockSpec((MEM_BATCH_TILE, rows, MEM_HEAD_DIM), lambda i: (i, 0, 0))
    kv_spec = pl.BlockSpec((None, MEM_BATCH_TILE, N_MEM, MEM_HEADS, MEM_HEAD_DIM),
                           lambda i: (layer, i, 0, 0, 0))
    return pl.pallas_call(
        _mem_attn_sample_kernel,
        grid=(n_seq // MEM_BATCH_TILE,),
        in_specs=[q_spec, kv_spec, kv_spec],
        out_specs=q_spec,
        out_shape=jax.ShapeDtypeStruct((n_seq, rows, MEM_HEAD_DIM), F32),
        compiler_params=_compiler_params(1),
        name="mem_attn_sample",
    )(qm3, cache_k, cache_v)


def _post_kernel(final_norm, x_ref, tok_ref, mo_ref, wo_ref, g_ref, wu_ref, wd_ref, gf_ref, o_ref):
    o_ref[...] = (x_ref[...]
                  + _dot(tok_ref[...].astype(BF16), wo_ref[0:D_MODEL, :])
                  + _dot(mo_ref[...].astype(BF16), wo_ref[D_MODEL:2 * D_MODEL, :]))
    x = o_ref[...]
    h = _rms(x, g_ref[...]).astype(BF16)
    for c in range(D_FF // D_MODEL):
        sl = slice(c * D_MODEL, (c + 1) * D_MODEL)
        a = jnp.square(jnp.maximum(_dot(h, wu_ref[:, sl]), 0.0)).astype(BF16)
        x = x + _dot(a, wd_ref[sl, :])
    o_ref[...] = x
    if final_norm:
        o_ref[...] = _rms(o_ref[...], gf_ref[...])


def _post(x2d, tok, mo, wo_b, wo_layer, g, wu_b, wd_b, layer, g_final, final_norm):
    rows = x2d.shape[0]
    tile_spec = pl.BlockSpec((TOKEN_TILE, D_MODEL), lambda i: (i, 0))
    return pl.pallas_call(
        functools.partial(_post_kernel, final_norm),
        grid=(rows // TOKEN_TILE,),
        in_specs=[
            tile_spec, tile_spec, tile_spec,
            _resident((2 * D_MODEL, D_MODEL), wo_layer),
            _resident((1, D_MODEL)),
            _resident((D_MODEL, D_FF), layer),
            _resident((D_FF, D_MODEL), layer),
            _resident((1, D_MODEL)),
        ],
        out_specs=tile_spec,
        out_shape=jax.ShapeDtypeStruct((rows, D_MODEL), F32),
        compiler_params=_compiler_params(1),
        name="post_final" if final_norm else "post",
    )(x2d, tok, mo, wo_b, g.reshape(1, D_MODEL), wu_b, wd_b, g_final.reshape(1, D_MODEL))


def _kv_prompt_kernel(x_ref, g_ref, w_ref, kaux_ref, k_ref, v_ref, kaug_ref, vt_ref, kmean_ref):
    h = _rms(x_ref[...], g_ref[...]).astype(BF16)
    kv = _dot(h, w_ref[...])
    k = kv[:, :D_MODEL]
    v = kv[:, D_MODEL:]
    k_ref[...] = k
    v_ref[...] = v
    vt_ref[...] = v.T.astype(BF16)
    kmean_ref[...] = jnp.sum(k, axis=0, keepdims=True) * (1.0 / MOBA_BLOCK)
    for hd in range(N_HEADS):
        src = slice(hd * HEAD_DIM, (hd + 1) * HEAD_DIM)
        kaug_ref[:, hd * AUG_WIDTH:hd * AUG_WIDTH + HEAD_DIM] = k[:, src].astype(BF16)
        kaug_ref[:, hd * AUG_WIDTH + HEAD_DIM:(hd + 1) * AUG_WIDTH] = kaux_ref[:, src]


def _kv_prompt(x2d, g, w_b, kaux):
    rows = x2d.shape[0]
    n_blocks = rows // MOBA_BLOCK
    seq_blocks = kaux.shape[0] // MOBA_BLOCK
    tile = lambda width: pl.BlockSpec((MOBA_BLOCK, width), lambda i: (i, 0))
    return pl.pallas_call(
        _kv_prompt_kernel,
        grid=(n_blocks,),
        in_specs=[
            tile(D_MODEL),
            _resident((1, D_MODEL)),
            _resident((D_MODEL, 2 * D_MODEL)),
            pl.BlockSpec((MOBA_BLOCK, D_MODEL), lambda i: (i % seq_blocks, 0)),
        ],
        out_specs=[
            tile(D_MODEL), tile(D_MODEL), tile(N_HEADS * AUG_WIDTH),
            pl.BlockSpec((None, D_MODEL, MOBA_BLOCK), lambda i: (i, 0, 0)),
            pl.BlockSpec((None, 1, D_MODEL), lambda i: (i, 0, 0)),
        ],
        out_shape=[
            jax.ShapeDtypeStruct((rows, D_MODEL), F32),
            jax.ShapeDtypeStruct((rows, D_MODEL), F32),
            jax.ShapeDtypeStruct((rows, N_HEADS * AUG_WIDTH), BF16),
            jax.ShapeDtypeStruct((n_blocks, D_MODEL, MOBA_BLOCK), BF16),
            jax.ShapeDtypeStruct((n_blocks, 1, D_MODEL), F32),
        ],
        compiler_params=_compiler_params(1),
        name="kv_prompt",
    )(x2d, g.reshape(1, D_MODEL), w_b, kaux)


def _kv_sample_kernel(x_ref, g_ref, w_ref, k_ref, v_ref):
    h = _rms(x_ref[...], g_ref[...]).astype(BF16)
    kv = _dot(h, w_ref[...])
    k_ref[...] = kv[:, :D_MODEL]
    v_ref[...] = kv[:, D_MODEL:]


def _kv_sample(x2d, g, w_b):
    rows = x2d.shape[0]
    tile_spec = pl.BlockSpec((TOKEN_TILE, D_MODEL), lambda i: (i, 0))
    out = jax.ShapeDtypeStruct((rows, D_MODEL), F32)
    return pl.pallas_call(
        _kv_sample_kernel,
        grid=(rows // TOKEN_TILE,),
        in_specs=[tile_spec, _resident((1, D_MODEL)), _resident((D_MODEL, 2 * D_MODEL))],
        out_specs=[tile_spec, tile_spec],
        out_shape=[out, out],
        compiler_params=_compiler_params(1),
        name="kv_sample",
    )(x2d, g.reshape(1, D_MODEL), w_b)


def _b_front_prompt_kernel(x_ref, g_ref, wqt_ref, wm_ref, km_ref, qaux_ref, qaug_ref, qm_ref):
    h = _rms(x_ref[...], g_ref[...]).astype(BF16)
    qm_ref[...] = _dot(h, wm_ref[...]).astype(BF16)
    qt = _dot_nt(wqt_ref[...], h)
    scale = HEAD_DIM ** -0.5
    t0 = pl.program_id(1) * TOKEN_TILE
    shape = (GATE_ROWS, TOKEN_TILE)
    no_mask_rows = jnp.zeros((HEAD_DIM - GATE_ROWS, TOKEN_TILE), F32)
    blk = lax.broadcasted_iota(jnp.int32, shape, 0)
    own = (t0 + lax.broadcasted_iota(jnp.int32, shape, 1)) // MOBA_BLOCK
    eligible = blk < own
    for hd in range(N_HEADS):
        qh = qt[hd * HEAD_DIM:(hd + 1) * HEAD_DIM, :]
        q_hi = qh.astype(BF16)
        q_lo = (qh - q_hi.astype(F32)).astype(BF16)
        gate = _dot(km_ref[hd], jnp.concatenate([q_hi, q_lo, q_hi], axis=0))
        gate = jnp.where(eligible, gate, -jnp.inf)
        masked = eligible
        for _ in range(MOBA_TOPK):
            best = jnp.max(gate, axis=0, keepdims=True)
            is_best = (gate == best) & (best > -jnp.inf)
            first = jnp.min(jnp.where(is_best, blk, GATE_ROWS), axis=0, keepdims=True)
            pick = blk == first
            masked = masked & jnp.logical_not(pick)
            gate = jnp.where(pick, -jnp.inf, gate)
        mask_rows = jnp.concatenate([jnp.where(masked, MASK_NEG, 0.0), no_mask_rows], axis=0)
        aux = mask_rows + qaux_ref[hd * HEAD_DIM:(hd + 1) * HEAD_DIM, :]
        qaug_ref[hd * AUG_WIDTH:hd * AUG_WIDTH + HEAD_DIM, :] = (qh * scale).astype(BF16)
        qaug_ref[hd * AUG_WIDTH + HEAD_DIM:(hd + 1) * AUG_WIDTH, :] = aux.astype(BF16)


def _b_front_prompt(x2d, batch, g, wq_t, wm, km3, qaux_t):
    rows = x2d.shape[0]
    seq = rows // batch
    tiles = seq // TOKEN_TILE
    tile_spec = pl.BlockSpec((TOKEN_TILE, D_MODEL), lambda b, i: (b * tiles + i, 0))
    return pl.pallas_call(
        _b_front_prompt_kernel,
        grid=(batch, tiles),
        in_specs=[
            tile_spec,
            _resident((1, D_MODEL)),
            _resident((D_MODEL, D_MODEL)),
            _resident((D_MODEL, D_MODEL)),
            pl.BlockSpec((None, N_HEADS, GATE_ROWS, 3 * HEAD_DIM), lambda b, i: (b, 0, 0, 0)),
            pl.BlockSpec((D_MODEL, TOKEN_TILE), lambda b, i: (0, i)),
        ],
        out_specs=[pl.BlockSpec((N_HEADS * AUG_WIDTH, TOKEN_TILE), lambda b, i: (b, i)), tile_spec],
        out_shape=[
            jax.ShapeDtypeStruct((batch * N_HEADS * AUG_WIDTH, seq), BF16),
            jax.ShapeDtypeStruct((rows, D_MODEL), BF16),
        ],
        compiler_params=_compiler_params(2),
        name="b_front_prompt",
    )(x2d, g.reshape(1, D_MODEL), wq_t, wm, km3, qaux_t)


def _b_front_sample_kernel(x_ref, g_ref, w_ref, q_ref, qm_ref):
    h = _rms(x_ref[...], g_ref[...]).astype(BF16)
    q_ref[...] = _dot(h, w_ref[:, :D_MODEL])
    qm_ref[...] = _dot(h, w_ref[:, D_MODEL:])


def _b_front_sample(x2d, g, w_b, layer):
    rows = x2d.shape[0]
    tile_spec = pl.BlockSpec((TOKEN_TILE, D_MODEL), lambda i: (i, 0))
    return pl.pallas_call(
        _b_front_sample_kernel,
        grid=(rows // TOKEN_TILE,),
        in_specs=[tile_spec, _resident((1, D_MODEL)), _resident((D_MODEL, 2 * D_MODEL), layer)],
        out_specs=[tile_spec, tile_spec],
        out_shape=[jax.ShapeDtypeStruct((rows, D_MODEL), F32),
                   jax.ShapeDtypeStruct((rows, D_MODEL), F32)],
        compiler_params=_compiler_params(1),
        name="b_front_sample",
    )(x2d, g.reshape(1, D_MODEL), w_b)


def _moba_prompt_kernel(q_ref, k_ref, v_ref, o_ref):
    j = pl.program_id(2)
    qt = q_ref[...]
    blocks_per_chunk = MOBA_CHUNK // MOBA_BLOCK
    own_chunks = MOBA_Q_TILE // MOBA_CHUNK
    assert own_chunks % 2 == 0

    def scores(c, causal=False):
        start = pl.multiple_of(c * MOBA_CHUNK, MOBA_CHUNK)
        s = _dot(k_ref[pl.ds(start, MOBA_CHUNK), :], qt)
        if causal:
            key = start + lax.broadcasted_iota(jnp.int32, s.shape, 0)
            query = j * MOBA_Q_TILE + lax.broadcasted_iota(jnp.int32, s.shape, 1)
            s = jnp.where(key <= query, s, MASK_NEG)
        return s

    ones_rows = jnp.ones((ONES_ROWS, MOBA_CHUNK), BF16)

    def values_t(c):
        parts = [v_ref[c * blocks_per_chunk + i] for i in range(blocks_per_chunk)]
        return jnp.concatenate([jnp.concatenate(parts, axis=1), ones_rows], axis=0)

    def probs(s, m):
        return jnp.exp(s - m).astype(BF16)

    first = j * own_chunks
    s = scores(first, causal=True)
    m = _reduce_rows(s, jnp.max)
    state_a = (m, _dot(values_t(first), probs(s, m)))
    for i in range(1, own_chunks):
        m, acc = state_a
        s = scores(first + i, causal=True)
        m_new = jnp.maximum(m, _reduce_rows(s, jnp.max))
        state_a = (m_new, jnp.exp(m - m_new) * acc + _dot(values_t(first + i), probs(s, m_new)))
    state_b = (jnp.full_like(m, MASK_NEG), jnp.zeros_like(state_a[1]))

    def pair(t, carry):
        (ma, acca), (mb, accb) = carry
        sa = scores(2 * t)
        ma_new = jnp.maximum(ma, _reduce_rows(sa, jnp.max))
        sb = scores(2 * t + 1)
        pa = probs(sa, ma_new)
        mb_new = jnp.maximum(mb, _reduce_rows(sb, jnp.max))
        acca = jnp.exp(ma - ma_new) * acca + _dot(values_t(2 * t), pa)
        pb = probs(sb, mb_new)
        accb = jnp.exp(mb - mb_new) * accb + _dot(values_t(2 * t + 1), pb)
        return (ma_new, acca), (mb_new, accb)

    state_a, state_b = lax.fori_loop(0, j * (own_chunks // 2), pair, (state_a, state_b))

    m = jnp.maximum(state_a[0], state_b[0])
    acc = jnp.exp(state_a[0] - m) * state_a[1] + jnp.exp(state_b[0] - m) * state_b[1]
    out_t = acc[:HEAD_DIM] / acc[HEAD_DIM:HEAD_DIM + 1]
    o_ref[...] = out_t.T.astype(o_ref.dtype)


def _moba_prompt(qaug_t, kaug, v_t, batch):
    rows = kaug.shape[0]
    seq = rows // batch
    assert seq % MOBA_Q_TILE == 0 and MOBA_Q_TILE % MOBA_CHUNK == 0 and MOBA_CHUNK % MOBA_BLOCK == 0
    q_tiles = seq // MOBA_Q_TILE
    seq_blocks = seq // MOBA_BLOCK
    return pl.pallas_call(
        _moba_prompt_kernel,
        grid=(batch, N_HEADS, q_tiles),
        in_specs=[
            pl.BlockSpec((AUG_WIDTH, MOBA_Q_TILE), lambda b, h, j: (b * N_HEADS + h, j)),
            pl.BlockSpec((seq, AUG_WIDTH), lambda b, h, j: (b, h)),
            pl.BlockSpec((seq_blocks, HEAD_DIM, MOBA_BLOCK), lambda b, h, j: (b, h, 0)),
        ],
        out_specs=pl.BlockSpec((MOBA_Q_TILE, HEAD_DIM), lambda b, h, j: (b * q_tiles + j, h)),
        out_shape=jax.ShapeDtypeStruct((rows, D_MODEL), BF16),
        compiler_params=_compiler_params(3),
        name="moba_prompt",
    )(qaug_t, kaug, v_t)


def _moba_sample_kernel(seq, past_len, n_pages, pt_ref, q_ref, kn_ref, vn_ref, slope_ref, *refs):
    del pt_ref
    k_pages = refs[:n_pages]
    v_pages = refs[n_pages:2 * n_pages]
    o_ref = refs[2 * n_pages]
    n_past = n_pages // 2
    rows = seq * N_HEADS
    cols = PAGE_SIZE * N_HEADS
    scale = HEAD_DIM ** -0.5

    q = q_ref[...]
    qb = q.astype(BF16)
    zq = jnp.zeros_like(qb)
    q_pair = jnp.concatenate([jnp.concatenate([qb, zq], axis=1),
                              jnp.concatenate([zq, qb], axis=1)], axis=0)

    def both(x):
        return jnp.concatenate([x, x], axis=0)

    slope = slope_ref[...]
    t_pos = (past_len + lax.broadcasted_iota(jnp.int32, (rows, 1), 0) // N_HEADS).astype(F32)
    r_id = lax.broadcasted_iota(jnp.int32, (2 * rows, cols), 0)
    c_id = lax.broadcasted_iota(jnp.int32, (2 * rows, cols), 1)
    same_head = (r_id % N_HEADS) == (c_id % N_HEADS)
    key_pos = ((r_id // rows) * PAGE_SIZE + c_id // N_HEADS).astype(F32)
    bias0 = jnp.where(same_head, -both(slope) * (both(t_pos) - key_pos), -jnp.inf)

    gates, scores = [], []
    for n in range(n_past):
        k0 = k_pages[2 * n][...]
        k1 = k_pages[2 * n + 1][...]
        k_mean = (jnp.sum(k0, axis=0) + jnp.sum(k1, axis=0)) * (1.0 / MOBA_BLOCK)
        gates.append(jnp.sum(q * jnp.concatenate([k_mean] * seq, axis=0), axis=1, keepdims=True))
        k_pair = jnp.concatenate([k0.reshape(cols, HEAD_DIM).astype(BF16),
                                  k1.reshape(cols, HEAD_DIM).astype(BF16)], axis=1)
        scores.append(_dot_nt(q_pair, k_pair) * scale + (bias0 + both(slope) * float(n * MOBA_BLOCK)))

    selected = []
    for n in range(n_past):
        rank = jnp.zeros((rows, 1), jnp.int32)
        for mth in range(n_past):
            if mth == n:
                continue
            ahead = (gates[mth] >= gates[n]) if mth < n else (gates[mth] > gates[n])
            rank = rank + ahead.astype(jnp.int32)
        selected.append(rank < MOBA_TOPK)

    r_own = lax.broadcasted_iota(jnp.int32, (rows, rows), 0)
    c_own = lax.broadcasted_iota(jnp.int32, (rows, rows), 1)
    own_ok = ((r_own % N_HEADS) == (c_own % N_HEADS)) & (c_own // N_HEADS <= r_own // N_HEADS)
    own_bias = -slope * (t_pos - (past_len + c_own // N_HEADS).astype(F32))
    s_own = jnp.where(own_ok, _dot_nt(qb, kn_ref[...].astype(BF16)) * scale + own_bias, -jnp.inf)

    m = jnp.max(s_own, axis=1, keepdims=True)
    for n in range(n_past):
        bm = jnp.max(scores[n], axis=1, keepdims=True)
        m = jnp.maximum(m, jnp.where(selected[n], jnp.maximum(bm[:rows], bm[rows:]), -jnp.inf))

    p_own = jnp.exp(s_own - m)
    l = jnp.sum(p_own, axis=1, keepdims=True)
    acc = _dot(p_own.astype(BF16), vn_ref[...].astype(BF16))
    for n in range(n_past):
        p = jnp.where(both(selected[n]), jnp.exp(scores[n] - both(m)), 0.0)
        ps = jnp.sum(p, axis=1, keepdims=True)
        l = l + ps[:rows] + ps[rows:]
        v_pair = jnp.concatenate([v_pages[2 * n][...].reshape(cols, HEAD_DIM).astype(BF16),
                                  v_pages[2 * n + 1][...].reshape(cols, HEAD_DIM).astype(BF16)], axis=1)
        pv = _dot(p.astype(BF16), v_pair)
        acc = acc + pv[:rows, :HEAD_DIM] + pv[rows:, HEAD_DIM:]
    o_ref[...] = acc / l


def _moba_sample(page_table, q3, k_new3, v_new3, cache_k, cache_v):
    n_seq, rows, _ = q3.shape
    seq = rows // N_HEADS
    n_pages = page_table.shape[1]
    past_len = n_pages * PAGE_SIZE
    assert MOBA_BLOCK == 2 * PAGE_SIZE and past_len % MOBA_BLOCK == 0 and seq <= MOBA_BLOCK
    assert past_len // MOBA_BLOCK >= MOBA_TOPK
    slopes = jnp.exp2(-ALIBI_MAX_EXP * jnp.arange(1, N_HEADS + 1, dtype=F32) / N_HEADS)
    slope_col = jnp.tile(slopes, seq).reshape(rows, 1)
    seq_spec = pl.BlockSpec((None, rows, HEAD_DIM), lambda b, pt: (b, 0, 0))

    def page_spec(p):
        return pl.BlockSpec((None, PAGE_SIZE, N_HEADS, HEAD_DIM), lambda b, pt: (pt[b, p], 0, 0, 0))

    grid_spec = pltpu.PrefetchScalarGridSpec(
        num_scalar_prefetch=1,
        grid=(n_seq,),
        in_specs=([seq_spec, seq_spec, seq_spec, pl.BlockSpec((rows, 1), lambda b, pt: (0, 0))]
                  + [page_spec(p) for p in range(n_pages)]
                  + [page_spec(p) for p in range(n_pages)]),
        out_specs=seq_spec,
    )
    return pl.pallas_call(
        functools.partial(_moba_sample_kernel, seq, past_len, n_pages),
        grid_spec=grid_spec,
        out_shape=jax.ShapeDtypeStruct((n_seq, rows, HEAD_DIM), F32),
        compiler_params=_compiler_params(1),
        name="moba_sample",
    )(page_table, q3, k_new3, v_new3, slope_col,
      *([cache_k] * n_pages), *([cache_v] * n_pages))


def _bf16_floor(v):
    bits = lax.bitcast_convert_type(v, jnp.uint32) & jnp.uint32(0xFFFF0000)
    return lax.bitcast_convert_type(bits, F32)


def _split3(v):
    hi = _bf16_floor(v)
    r1 = v - hi
    mid = _bf16_floor(r1)
    return [hi, mid, r1 - mid]


def _moba_aux(seq):
    assert seq // MOBA_BLOCK <= N_BLOCK_COLS and N_BLOCK_COLS + 6 <= HEAD_DIM
    slopes = jnp.exp2(-ALIBI_MAX_EXP * jnp.arange(1, N_HEADS + 1, dtype=F32) / N_HEADS)
    pos = jnp.arange(seq, dtype=F32)
    sp = slopes[None, :] * pos[:, None]
    ones = jnp.ones((seq, N_HEADS), F32)
    onehot = (jnp.arange(seq)[:, None] // MOBA_BLOCK == jnp.arange(N_BLOCK_COLS)[None, :]).astype(F32)
    onehot = jnp.broadcast_to(onehot[:, None, :], (seq, N_HEADS, N_BLOCK_COLS))
    pad = jnp.zeros((seq, N_HEADS, HEAD_DIM - N_BLOCK_COLS - 6), F32)
    k_cols = jnp.stack(_split3(sp) + [ones] * 3, axis=-1)
    q_cols = jnp.stack([ones] * 3 + _split3(-sp), axis=-1)
    k_aux = jnp.concatenate([onehot, k_cols, pad], axis=-1).reshape(seq, D_MODEL).astype(BF16)
    q_aux = jnp.concatenate([jnp.zeros_like(onehot), q_cols, pad], axis=-1).reshape(seq, D_MODEL)
    return k_aux, q_aux.T


def _gate_means(kmean, batch):
    blocks = kmean.shape[0] // batch
    assert blocks <= N_BLOCK_COLS <= GATE_ROWS
    km = kmean.reshape(batch, blocks, N_HEADS, HEAD_DIM).transpose(0, 2, 1, 3)
    km = jnp.pad(km, ((0, 0), (0, 0), (0, GATE_ROWS - blocks), (0, 0)))
    hi = _bf16_floor(km)
    lo = km - hi
    return jnp.concatenate([hi, hi, lo], axis=-1).astype(BF16)


def kernel(x_prompt, x_sample, state_conv, cache_k, cache_v, cache_mem_k, cache_mem_v, page_table,
           mem_prompt, g_mix, w_in_a, conv_w, w_out_a, g_kv, w_kv, w_in_b, w_out_b,
           g_mem, w_mem_kv, g_mlp, w_up, w_down, g_final):
    batch, seq, _ = x_prompt.shape
    n_seq, dec_seq, _ = x_sample.shape
    assert seq % TOKEN_TILE == 0 and (n_seq * dec_seq) % TOKEN_TILE == 0
    assert TOKEN_TILE % MOBA_BLOCK == 0 and n_seq % MEM_BATCH_TILE == 0

    w_in_a_b, w_out_a_b, w_kv_b = w_in_a.astype(BF16), w_out_a.astype(BF16), w_kv.astype(BF16)
    w_in_b_b, w_out_b_b = w_in_b.astype(BF16), w_out_b.astype(BF16)
    w_mem_kv_b, w_up_b, w_down_b = w_mem_kv.astype(BF16), w_up.astype(BF16), w_down.astype(BF16)

    mem_k, mem_v, mem_kb, mem_vb = _mem_kv(mem_prompt.reshape(batch * N_MEM, D_MODEL), g_mem, w_mem_kv_b)
    k_aux, q_aux_t = _moba_aux(seq)

    xp = x_prompt.reshape(batch * seq, D_MODEL)
    xs = x_sample.reshape(n_seq * dec_seq, D_MODEL)
    conv_p, conv_s = [], []
    for layer in range(DEPTH):
        if layer < N_A_LAYERS:
            tok_p, qm_p, ulast = _a_front_prompt(xp, batch, g_mix[layer], w_in_a_b, layer, conv_w[layer])
            conv_p.append(ulast.reshape(batch, -1, SUBLANES, D_MODEL)[:, -1, SUBLANES - (CONV_WIDTH - 1):])
            tok_s, qm_s, u_s = _a_front_sample(xs, dec_seq, g_mix[layer], w_in_a_b, layer, conv_w[layer],
                                               state_conv[layer])
            conv_s.append(u_s.reshape(n_seq, dec_seq, D_MODEL)[:, dec_seq - (CONV_WIDTH - 1):])
            w_out, w_out_layer = w_out_a_b, layer
        else:
            bi = layer - N_A_LAYERS
            qaug_t, qm_p = _b_front_prompt(xp, batch, g_mix[layer], w_in_b_b[bi, :, :D_MODEL].T,
                                           w_in_b_b[bi, :, D_MODEL:], km3, q_aux_t)
            tok_p = _moba_prompt(qaug_t, kaug, v_t, batch)
            q_s, qm_s = _b_front_sample(xs, g_mix[layer], w_in_b_b, bi)
            tok_s = _moba_sample(page_table, q_s.reshape(n_seq, dec_seq * N_HEADS, HEAD_DIM), k_s3, v_s3,
                                 cache_k, cache_v).reshape(n_seq * dec_seq, D_MODEL)
            w_out, w_out_layer = w_out_b_b, bi
        mo_p = _mem_attn_prompt(qm_p, batch, mem_kb[layer], mem_vb[layer])
        mo_s = _mem_attn_sample(qm_s.reshape(n_seq, dec_seq * MEM_HEADS, MEM_HEAD_DIM), layer, cache_mem_k,
                                cache_mem_v).reshape(n_seq * dec_seq, D_MODEL)
        last = layer == DEPTH - 1
        xp = _post(xp, tok_p, mo_p, w_out, w_out_layer, g_mlp[layer], w_up_b, w_down_b, layer, g_final, last)
        xs = _post(xs, tok_s, mo_s, w_out, w_out_layer, g_mlp[layer], w_up_b, w_down_b, layer, g_final, last)
        if layer == N_A_LAYERS - 1:
            k_p, v_p, kaug, v_t, kmean = _kv_prompt(xp, g_kv, w_kv_b, k_aux)
            km3 = _gate_means(kmean, batch)
            k_s, v_s = _kv_sample(xs, g_kv, w_kv_b)
            k_s3 = k_s.reshape(n_seq, dec_seq * N_HEADS, HEAD_DIM)
            v_s3 = v_s.reshape(n_seq, dec_seq * N_HEADS, HEAD_DIM)

    mem_shape = (DEPTH, batch, N_MEM, MEM_HEADS, MEM_HEAD_DIM)
    return (xp.reshape(batch, seq, D_MODEL),
            xs.reshape(n_seq, dec_seq, D_MODEL),
            jnp.stack(conv_p),
            jnp.stack(conv_s),
            k_p.reshape(batch, seq, N_HEADS, HEAD_DIM),
            v_p.reshape(batch, seq, N_HEADS, HEAD_DIM),
            k_s.reshape(n_seq, dec_seq, N_HEADS, HEAD_DIM),
            v_s.reshape(n_seq, dec_seq, N_HEADS, HEAD_DIM),
            mem_k.reshape(mem_shape),
            mem_v.reshape(mem_shape))
```

```python
import functools

import jax
import jax.numpy as jnp
from jax import lax
from jax.experimental import pallas as pl
from jax.experimental.pallas import tpu as pltpu

F32 = jnp.float32
BF16 = jnp.bfloat16

D_MODEL = 1024
DEPTH = 4
N_A_LAYERS = DEPTH // 2
CONV_WIDTH = 3
N_HEADS = 8
HEAD_DIM = D_MODEL // N_HEADS
MOBA_BLOCK = 256
MOBA_TOPK = 3
N_MEM = 256
MEM_HEADS = 4
MEM_HEAD_DIM = D_MODEL // MEM_HEADS
D_FF = 4 * D_MODEL
ALIBI_MAX_EXP = 8.0
EPS = 1e-5
PAGE_SIZE = 128

LANES = 128
SUBLANES = 8
VMEM_LIMIT_BYTES = 56 * 1024 * 1024

TOKEN_TILE = 512
AUG_WIDTH = 2 * HEAD_DIM
MOBA_CHUNK = 2 * MOBA_BLOCK
MOBA_Q_TILE = 2 * MOBA_CHUNK
ONES_ROWS = 16
N_BLOCK_COLS = 40
GATE_ROWS = 48
ALIBI_COL0 = N_BLOCK_COLS
ALIBI_ROWS = 16
POS_RADIX = 128
MASK_NEG = -(2.0 ** 100)
MEM_BATCH_TILE = 4


def _compiler_params(n_grid_dims):
    return pltpu.CompilerParams(
        dimension_semantics=("arbitrary",) * n_grid_dims,
        vmem_limit_bytes=VMEM_LIMIT_BYTES)


def _resident(shape, layer=None):
    zeros = (0,) * len(shape)
    if layer is None:
        return pl.BlockSpec(shape, lambda *_: zeros, pipeline_mode=pl.Buffered(1))
    return pl.BlockSpec((None,) + tuple(shape), lambda *_: (layer,) + zeros, pipeline_mode=pl.Buffered(1))


def _rms(x, g):
    return x * lax.rsqrt(jnp.mean(x * x, axis=-1, keepdims=True) + EPS) * g


def _dot(a, b):
    return jnp.dot(a, b, preferred_element_type=F32)


def _dot_nt(a, b):
    return lax.dot_general(a, b, (((1,), (1,)), ((), ())), preferred_element_type=F32)


def _reduce_rows(x, op):
    rows, cols = x.shape
    groups = 8
    partial = op(x.reshape(groups, rows // groups, cols), axis=0)
    return op(partial, axis=0, keepdims=True)


def _softmax_pv(s, v):
    m = jnp.max(s, axis=-1, keepdims=True)
    p = jnp.exp(s - m)
    l = jnp.sum(p, axis=-1, keepdims=True)
    return _dot(p.astype(BF16), v) / l


def _mem_kv_kernel(mem_ref, g_ref, w_ref, k_ref, v_ref, kb_ref, vb_ref):
    hn = _rms(mem_ref[...], g_ref[...]).astype(BF16)
    kv = _dot(hn, w_ref[...])
    k = kv[:, :D_MODEL]
    v = kv[:, D_MODEL:]
    k_ref[...] = k
    v_ref[...] = v
    kb_ref[...] = k.astype(BF16)
    vb_ref[...] = v.astype(BF16)


def _mem_kv(mem2d, g_mem, w_mem_kv_b):
    rows = mem2d.shape[0]
    out_f = jax.ShapeDtypeStruct((DEPTH, rows, D_MODEL), F32)
    out_b = jax.ShapeDtypeStruct((DEPTH, rows, D_MODEL), BF16)
    blk = pl.BlockSpec((None, rows, D_MODEL), lambda l: (l, 0, 0))
    return pl.pallas_call(
        _mem_kv_kernel,
        grid=(DEPTH,),
        in_specs=[
            pl.BlockSpec((rows, D_MODEL), lambda l: (0, 0)),
            pl.BlockSpec((None, 1, D_MODEL), lambda l: (l, 0, 0)),
            pl.BlockSpec((None, D_MODEL, 2 * D_MODEL), lambda l: (l, 0, 0)),
        ],
        out_specs=[blk, blk, blk, blk],
        out_shape=[out_f, out_f, out_b, out_b],
        compiler_params=_compiler_params(1),
        name="mem_kv",
    )(mem2d, g_mem.reshape(DEPTH, 1, D_MODEL), w_mem_kv_b)


def _conv_gate(h, w_ref, cw_ref, fix_prev):
    cg = _dot(h, w_ref[:, D_MODEL:2 * D_MODEL])
    hv = _dot(h, w_ref[:, 2 * D_MODEL:3 * D_MODEL])
    u = cg * hv
    prev1, prev2 = fix_prev(pltpu.roll(u, 1, 0), pltpu.roll(u, 2, 0))
    cw = cw_ref[...]
    z = cw[0:1, :] * prev2 + cw[1:2, :] * prev1 + cw[2:3, :] * u
    bg = _dot(h, w_ref[:, 0:D_MODEL])
    qm = _dot(h, w_ref[:, 3 * D_MODEL:4 * D_MODEL])
    return bg * z, qm, u


def _a_front_prompt_kernel(x_ref, g_ref, w_ref, cw_ref, tok_ref, qm_ref, ulast_ref, carry_ref):
    @pl.when(pl.program_id(1) == 0)
    def _():
        carry_ref[...] = jnp.zeros_like(carry_ref)

    h = _rms(x_ref[...], g_ref[...]).astype(BF16)
    carry = carry_ref[...]

    def fix_prev(prev1, prev2):
        row = lax.broadcasted_iota(jnp.int32, prev1.shape, 0)
        c_m1 = carry[SUBLANES - 1:SUBLANES, :]
        c_m2 = carry[SUBLANES - 2:SUBLANES - 1, :]
        prev1 = jnp.where(row < 1, c_m1, prev1)
        prev2 = jnp.where(row < 1, c_m2, jnp.where(row < 2, c_m1, prev2))
        return prev1, prev2

    tok, qm, u = _conv_gate(h, w_ref, cw_ref, fix_prev)
    tok_ref[...] = tok.astype(BF16)
    qm_ref[...] = qm.astype(BF16)
    tail = u[TOKEN_TILE - SUBLANES:, :]
    carry_ref[...] = tail
    ulast_ref[...] = tail


def _a_front_prompt(x2d, batch, g, w_b, layer, conv_w):
    rows = x2d.shape[0]
    tiles = rows // batch // TOKEN_TILE
    tile_spec = pl.BlockSpec((TOKEN_TILE, D_MODEL), lambda b, i: (b * tiles + i, 0))
    return pl.pallas_call(
        _a_front_prompt_kernel,
        grid=(batch, tiles),
        in_specs=[
            tile_spec,
            _resident((1, D_MODEL)),
            _resident((D_MODEL, 4 * D_MODEL), layer),
            _resident((CONV_WIDTH, D_MODEL)),
        ],
        out_specs=[
            tile_spec,
            tile_spec,
            pl.BlockSpec((SUBLANES, D_MODEL), lambda b, i: (b * tiles + i, 0)),
        ],
        out_shape=[
            jax.ShapeDtypeStruct((rows, D_MODEL), BF16),
            jax.ShapeDtypeStruct((rows, D_MODEL), BF16),
            jax.ShapeDtypeStruct((batch * tiles * SUBLANES, D_MODEL), F32),
        ],
        scratch_shapes=[pltpu.VMEM((SUBLANES, D_MODEL), F32)],
        compiler_params=_compiler_params(2),
        name="a_front_prompt",
    )(x2d, g.reshape(1, D_MODEL), w_b, conv_w)


def _a_front_sample_kernel(seq, x_ref, g_ref, w_ref, cw_ref, s1_ref, s2_ref, tok_ref, qm_ref, u_ref):
    h = _rms(x_ref[...], g_ref[...]).astype(BF16)

    def fix_prev(prev1, prev2):
        pos = lax.broadcasted_iota(jnp.int32, prev1.shape, 0) % seq
        return (jnp.where(pos >= 1, prev1, s1_ref[...]),
                jnp.where(pos >= 2, prev2, s2_ref[...]))

    tok, qm, u = _conv_gate(h, w_ref, cw_ref, fix_prev)
    tok_ref[...] = tok.astype(BF16)
    qm_ref[...] = qm
    u_ref[...] = u


def _a_front_sample(x2d, seq, g, w_b, layer, conv_w, state):
    rows = x2d.shape[0]
    n_seq = rows // seq
    assert seq >= CONV_WIDTH - 1
    zeros = jnp.zeros((n_seq, seq - 1, D_MODEL), F32)
    s1 = jnp.concatenate([state[:, 1:2], zeros], axis=1).reshape(rows, D_MODEL)
    s2 = jnp.concatenate([state, zeros[:, 1:]], axis=1).reshape(rows, D_MODEL)
    full = pl.BlockSpec((rows, D_MODEL), lambda i: (0, 0))
    return pl.pallas_call(
        functools.partial(_a_front_sample_kernel, seq),
        grid=(1,),
        in_specs=[full, _resident((1, D_MODEL)), _resident((D_MODEL, 4 * D_MODEL), layer),
                  _resident((CONV_WIDTH, D_MODEL)), full, full],
        out_specs=[full, full, full],
        out_shape=[
            jax.ShapeDtypeStruct((rows, D_MODEL), BF16),
            jax.ShapeDtypeStruct((rows, D_MODEL), F32),
            jax.ShapeDtypeStruct((rows, D_MODEL), F32),
        ],
        compiler_params=_compiler_params(1),
        name="a_front_sample",
    )(x2d, g.reshape(1, D_MODEL), w_b, conv_w, s1, s2)


def _mem_attn_prompt_kernel(q_ref, k_ref, v_ref, o_ref):
    scale = MEM_HEAD_DIM ** -0.5
    for hd in range(MEM_HEADS):
        sl = slice(hd * MEM_HEAD_DIM, (hd + 1) * MEM_HEAD_DIM)
        s = _dot_nt(q_ref[:, sl], k_ref[:, sl]) * scale
        o_ref[:, sl] = _softmax_pv(s, v_ref[:, sl]).astype(o_ref.dtype)


def _mem_attn_prompt(qm, batch, k_b, v_b):
    rows = qm.shape[0]
    tiles = rows // batch // TOKEN_TILE
    tile_spec = pl.BlockSpec((TOKEN_TILE, D_MODEL), lambda b, i: (b * tiles + i, 0))
    kv_spec = pl.BlockSpec((N_MEM, D_MODEL), lambda b, i: (b, 0))
    return pl.pallas_call(
        _mem_attn_prompt_kernel,
        grid=(batch, tiles),
        in_specs=[tile_spec, kv_spec, kv_spec],
        out_specs=tile_spec,
        out_shape=jax.ShapeDtypeStruct((rows, D_MODEL), BF16),
        compiler_params=_compiler_params(2),
        name="mem_attn_prompt",
    )(qm, k_b, v_b)


def _mem_attn_sample_kernel(q_ref, k_ref, v_ref, o_ref):
    scale = MEM_HEAD_DIM ** -0.5
    rows = q_ref.shape[1]
    cols = N_MEM * MEM_HEADS
    same_head = ((lax.broadcasted_iota(jnp.int32, (rows, cols), 0) % MEM_HEADS)
                 == (lax.broadcasted_iota(jnp.int32, (rows, cols), 1) % MEM_HEADS))
    for bb in range(MEM_BATCH_TILE):
        k = k_ref[bb].reshape(cols, MEM_HEAD_DIM).astype(BF16)
        v = v_ref[bb].reshape(cols, MEM_HEAD_DIM).astype(BF16)
        s = jnp.where(same_head, _dot_nt(q_ref[bb].astype(BF16), k) * scale, -jnp.inf)
        o_ref[bb] = _softmax_pv(s, v)


def _mem_attn_sample(qm3, layer, cache_k, cache_v):
    n_seq, rows, _ = qm3.shape
    q_spec = pl.BlockSpec((MEM_BATCH_TILE, rows, MEM_HEAD_DIM), lambda i: (i, 0, 0))
    kv_spec = pl.BlockSpec((None, MEM_BATCH_TILE, N_MEM, MEM_HEADS, MEM_HEAD_DIM),
                           lambda i: (layer, i, 0, 0, 0))
    return pl.pallas_call(
        _mem_attn_sample_kernel,
        grid=(n_seq // MEM_BATCH_TILE,),
        in_specs=[q_spec, kv_spec, kv_spec],
        out_specs=q_spec,
        out_shape=jax.ShapeDtypeStruct((n_seq, rows, MEM_HEAD_DIM), F32),
        compiler_params=_compiler_params(1),
        name="mem_attn_sample",
    )(qm3, cache_k, cache_v)


def _post_kernel(final_norm, x_ref, tok_ref, mo_ref, wo_ref, g_ref, wu_ref, wd_ref, gf_ref, o_ref):
    o_ref[...] = (x_ref[...]
                  + _dot(tok_ref[...].astype(BF16), wo_ref[0:D_MODEL, :])
                  + _dot(mo_ref[...].astype(BF16), wo_ref[D_MODEL:2 * D_MODEL, :]))
    x = o_ref[...]
    h = _rms(x, g_ref[...]).astype(BF16)
    for c in range(D_FF // D_MODEL):
        sl = slice(c * D_MODEL, (c + 1) * D_MODEL)
        a = jnp.square(jnp.maximum(_dot(h, wu_ref[:, sl]), 0.0)).astype(BF16)
        x = x + _dot(a, wd_ref[sl, :])
    o_ref[...] = x
    if final_norm:
        o_ref[...] = _rms(o_ref[...], gf_ref[...])


def _post(x2d, tok, mo, wo_b, wo_layer, g, wu_b, wd_b, layer, g_final, final_norm):
    rows = x2d.shape[0]
    tile_spec = pl.BlockSpec((TOKEN_TILE, D_MODEL), lambda i: (i, 0))
    return pl.pallas_call(
        functools.partial(_post_kernel, final_norm),
        grid=(rows // TOKEN_TILE,),
        in_specs=[
            tile_spec, tile_spec, tile_spec,
            _resident((2 * D_MODEL, D_MODEL), wo_layer),
            _resident((1, D_MODEL)),
            _resident((D_MODEL, D_FF), layer),
            _resident((D_FF, D_MODEL), layer),
            _resident((1, D_MODEL)),
        ],
        out_specs=tile_spec,
        out_shape=jax.ShapeDtypeStruct((rows, D_MODEL), F32),
        compiler_params=_compiler_params(1),
        name="post_final" if final_norm else "post",
    )(x2d, tok, mo, wo_b, g.reshape(1, D_MODEL), wu_b, wd_b, g_final.reshape(1, D_MODEL))


def _pos_digits(pos):
    low = pos & (POS_RADIX - 1)
    return (pos - low).astype(F32), low.astype(F32)


def _kv_prompt_kernel(seq_blocks, x_ref, g_ref, w_ref, krow_ref, k_ref, v_ref, kaug_ref, vt_ref, kmean_ref):
    h = _rms(x_ref[...], g_ref[...]).astype(BF16)
    kv = _dot(h, w_ref[...])
    k = kv[:, :D_MODEL]
    v = kv[:, D_MODEL:]
    k_ref[...] = k
    v_ref[...] = v
    vt_ref[...] = v.T.astype(BF16)
    kmean_ref[...] = jnp.sum(k, axis=0, keepdims=True) * (1.0 / MOBA_BLOCK)
    blk = pl.program_id(0) % seq_blocks
    col = lax.broadcasted_iota(jnp.int32, (MOBA_BLOCK, HEAD_DIM), 1)
    pos_hi, pos_lo = _pos_digits(blk * MOBA_BLOCK + lax.broadcasted_iota(jnp.int32, (MOBA_BLOCK, HEAD_DIM), 0))
    shared = jnp.where(col == blk, 1.0, 0.0)
    shared = jnp.where((col >= ALIBI_COL0) & (col < ALIBI_COL0 + 3), pos_hi, shared)
    shared = jnp.where((col >= ALIBI_COL0 + 3) & (col < ALIBI_COL0 + 6), pos_lo, shared)
    for hd in range(N_HEADS):
        src = slice(hd * HEAD_DIM, (hd + 1) * HEAD_DIM)
        kaug_ref[:, hd * AUG_WIDTH:hd * AUG_WIDTH + HEAD_DIM] = k[:, src].astype(BF16)
        kaug_ref[:, hd * AUG_WIDTH + HEAD_DIM:(hd + 1) * AUG_WIDTH] = (
            shared + krow_ref[hd:hd + 1, :]).astype(BF16)


def _kv_prompt(x2d, batch, g, w_b, k_rows):
    rows = x2d.shape[0]
    n_blocks = rows // MOBA_BLOCK
    seq_blocks = n_blocks // batch
    assert seq_blocks <= N_BLOCK_COLS
    tile = lambda width: pl.BlockSpec((MOBA_BLOCK, width), lambda i: (i, 0))
    return pl.pallas_call(
        functools.partial(_kv_prompt_kernel, seq_blocks),
        grid=(n_blocks,),
        in_specs=[
            tile(D_MODEL),
            _resident((1, D_MODEL)),
            _resident((D_MODEL, 2 * D_MODEL)),
            _resident((N_HEADS, HEAD_DIM)),
        ],
        out_specs=[
            tile(D_MODEL), tile(D_MODEL), tile(N_HEADS * AUG_WIDTH),
            pl.BlockSpec((None, D_MODEL, MOBA_BLOCK), lambda i: (i, 0, 0)),
            pl.BlockSpec((None, 1, D_MODEL), lambda i: (i, 0, 0)),
        ],
        out_shape=[
            jax.ShapeDtypeStruct((rows, D_MODEL), F32),
            jax.ShapeDtypeStruct((rows, D_MODEL), F32),
            jax.ShapeDtypeStruct((rows, N_HEADS * AUG_WIDTH), BF16),
            jax.ShapeDtypeStruct((n_blocks, D_MODEL, MOBA_BLOCK), BF16),
            jax.ShapeDtypeStruct((n_blocks, 1, D_MODEL), F32),
        ],
        compiler_params=_compiler_params(1),
        name="kv_prompt",
    )(x2d, g.reshape(1, D_MODEL), w_b, k_rows)


def _kv_sample_kernel(x_ref, g_ref, w_ref, k_ref, v_ref):
    h = _rms(x_ref[...], g_ref[...]).astype(BF16)
    kv = _dot(h, w_ref[...])
    k_ref[...] = kv[:, :D_MODEL]
    v_ref[...] = kv[:, D_MODEL:]


def _kv_sample(x2d, g, w_b):
    rows = x2d.shape[0]
    tile_spec = pl.BlockSpec((TOKEN_TILE, D_MODEL), lambda i: (i, 0))
    out = jax.ShapeDtypeStruct((rows, D_MODEL), F32)
    return pl.pallas_call(
        _kv_sample_kernel,
        grid=(rows // TOKEN_TILE,),
        in_specs=[tile_spec, _resident((1, D_MODEL)), _resident((D_MODEL, 2 * D_MODEL))],
        out_specs=[tile_spec, tile_spec],
        out_shape=[out, out],
        compiler_params=_compiler_params(1),
        name="kv_sample",
    )(x2d, g.reshape(1, D_MODEL), w_b)


def _b_front_prompt_kernel(x_ref, g_ref, wqt_ref, wm_ref, km_ref, qrow_ref, qaug_ref, qm_ref):
    h = _rms(x_ref[...], g_ref[...]).astype(BF16)
    qm_ref[...] = _dot(h, wm_ref[...]).astype(BF16)
    qt = _dot_nt(wqt_ref[...], h)
    scale = HEAD_DIM ** -0.5
    t0 = pl.program_id(1) * TOKEN_TILE
    shape = (GATE_ROWS, TOKEN_TILE)
    a_shape = (ALIBI_ROWS, TOKEN_TILE)
    a_row = lax.broadcasted_iota(jnp.int32, a_shape, 0)
    t_hi, t_lo = _pos_digits(t0 + lax.broadcasted_iota(jnp.int32, a_shape, 1))
    t_rows = jnp.where((a_row >= 6) & (a_row < 9), -t_hi, jnp.where((a_row >= 9) & (a_row < 12), -t_lo, 0.0))
    tail_rows = jnp.zeros((HEAD_DIM - ALIBI_COL0 - ALIBI_ROWS, TOKEN_TILE), F32)
    blk = lax.broadcasted_iota(jnp.int32, shape, 0)
    own = (t0 + lax.broadcasted_iota(jnp.int32, shape, 1)) // MOBA_BLOCK
    eligible = blk < own
    for hd in range(N_HEADS):
        qh = qt[hd * HEAD_DIM:(hd + 1) * HEAD_DIM, :]
        q_hi = qh.astype(BF16)
        q_lo = (qh - q_hi.astype(F32)).astype(BF16)
        gate = _dot(km_ref[hd], jnp.concatenate([q_hi, q_lo, q_hi], axis=0))
        gate = jnp.where(eligible, gate, -jnp.inf)
        masked = eligible
        for _ in range(MOBA_TOPK):
            best = jnp.max(gate, axis=0, keepdims=True)
            is_best = (gate == best) & (best > -jnp.inf)
            first = jnp.min(jnp.where(is_best, blk, GATE_ROWS), axis=0, keepdims=True)
            pick = blk == first
            masked = masked & jnp.logical_not(pick)
            gate = jnp.where(pick, -jnp.inf, gate)
        mask_rows = jnp.where(masked, MASK_NEG, 0.0)[:ALIBI_COL0]
        alibi_rows = t_rows + jnp.concatenate([qrow_ref[hd]] * (TOKEN_TILE // LANES), axis=1)
        aux = jnp.concatenate([mask_rows, alibi_rows, tail_rows], axis=0)
        qaug_ref[hd * AUG_WIDTH:hd * AUG_WIDTH + HEAD_DIM, :] = (qh * scale).astype(BF16)
        qaug_ref[hd * AUG_WIDTH + HEAD_DIM:(hd + 1) * AUG_WIDTH, :] = aux.astype(BF16)


def _b_front_prompt(x2d, batch, g, wq_t, wm, km3, q_rows):
    rows = x2d.shape[0]
    seq = rows // batch
    tiles = seq // TOKEN_TILE
    tile_spec = pl.BlockSpec((TOKEN_TILE, D_MODEL), lambda b, i: (b * tiles + i, 0))
    return pl.pallas_call(
        _b_front_prompt_kernel,
        grid=(batch, tiles),
        in_specs=[
            tile_spec,
            _resident((1, D_MODEL)),
            _resident((D_MODEL, D_MODEL)),
            _resident((D_MODEL, D_MODEL)),
            pl.BlockSpec((None, N_HEADS, GATE_ROWS, 3 * HEAD_DIM), lambda b, i: (b, 0, 0, 0)),
            _resident((N_HEADS, ALIBI_ROWS, LANES)),
        ],
        out_specs=[pl.BlockSpec((N_HEADS * AUG_WIDTH, TOKEN_TILE), lambda b, i: (b, i)), tile_spec],
        out_shape=[
            jax.ShapeDtypeStruct((batch * N_HEADS * AUG_WIDTH, seq), BF16),
            jax.ShapeDtypeStruct((rows, D_MODEL), BF16),
        ],
        compiler_params=_compiler_params(2),
        name="b_front_prompt",
    )(x2d, g.reshape(1, D_MODEL), wq_t, wm, km3, q_rows)


def _b_front_sample_kernel(x_ref, g_ref, w_ref, q_ref, qm_ref):
    h = _rms(x_ref[...], g_ref[...]).astype(BF16)
    q_ref[...] = _dot(h, w_ref[:, :D_MODEL])
    qm_ref[...] = _dot(h, w_ref[:, D_MODEL:])


def _b_front_sample(x2d, g, w_b, layer):
    rows = x2d.shape[0]
    tile_spec = pl.BlockSpec((TOKEN_TILE, D_MODEL), lambda i: (i, 0))
    return pl.pallas_call(
        _b_front_sample_kernel,
        grid=(rows // TOKEN_TILE,),
        in_specs=[tile_spec, _resident((1, D_MODEL)), _resident((D_MODEL, 2 * D_MODEL), layer)],
        out_specs=[tile_spec, tile_spec],
        out_shape=[jax.ShapeDtypeStruct((rows, D_MODEL), F32),
                   jax.ShapeDtypeStruct((rows, D_MODEL), F32)],
        compiler_params=_compiler_params(1),
        name="b_front_sample",
    )(x2d, g.reshape(1, D_MODEL), w_b)


def _moba_prompt_kernel(q_ref, k_ref, v_ref, o_ref):
    j = pl.program_id(2)
    qt = q_ref[...]
    blocks_per_chunk = MOBA_CHUNK // MOBA_BLOCK
    own_chunks = MOBA_Q_TILE // MOBA_CHUNK
    assert own_chunks % 2 == 0

    def scores(c, causal=False):
        start = pl.multiple_of(c * MOBA_CHUNK, MOBA_CHUNK)
        s = _dot(k_ref[pl.ds(start, MOBA_CHUNK), :], qt)
        if causal:
            key = start + lax.broadcasted_iota(jnp.int32, s.shape, 0)
            query = j * MOBA_Q_TILE + lax.broadcasted_iota(jnp.int32, s.shape, 1)
            s = jnp.where(key <= query, s, MASK_NEG)
        return s

    ones_rows = jnp.ones((ONES_ROWS, MOBA_CHUNK), BF16)

    def values_t(c):
        parts = [v_ref[c * blocks_per_chunk + i] for i in range(blocks_per_chunk)]
        return jnp.concatenate([jnp.concatenate(parts, axis=1), ones_rows], axis=0)

    def probs(s, m):
        return jnp.exp(s - m).astype(BF16)

    first = j * own_chunks
    s = scores(first, causal=True)
    m = _reduce_rows(s, jnp.max)
    state_a = (m, _dot(values_t(first), probs(s, m)))
    for i in range(1, own_chunks):
        m, acc = state_a
        s = scores(first + i, causal=True)
        m_new = jnp.maximum(m, _reduce_rows(s, jnp.max))
        state_a = (m_new, jnp.exp(m - m_new) * acc + _dot(values_t(first + i), probs(s, m_new)))
    state_b = (jnp.full_like(m, MASK_NEG), jnp.zeros_like(state_a[1]))

    def pair(t, carry):
        (ma, acca), (mb, accb) = carry
        sa = scores(2 * t)
        ma_new = jnp.maximum(ma, _reduce_rows(sa, jnp.max))
        sb = scores(2 * t + 1)
        pa = probs(sa, ma_new)
        mb_new = jnp.maximum(mb, _reduce_rows(sb, jnp.max))
        acca = jnp.exp(ma - ma_new) * acca + _dot(values_t(2 * t), pa)
        pb = probs(sb, mb_new)
        accb = jnp.exp(mb - mb_new) * accb + _dot(values_t(2 * t + 1), pb)
        return (ma_new, acca), (mb_new, accb)

    state_a, state_b = lax.fori_loop(0, j * (own_chunks // 2), pair, (state_a, state_b))

    m = jnp.maximum(state_a[0], state_b[0])
    acc = jnp.exp(state_a[0] - m) * state_a[1] + jnp.exp(state_b[0] - m) * state_b[1]
    out_t = acc[:HEAD_DIM] / acc[HEAD_DIM:HEAD_DIM + 1]
    o_ref[...] = out_t.T.astype(o_ref.dtype)


def _moba_prompt(qaug_t, kaug, v_t, batch):
    rows = kaug.shape[0]
    seq = rows // batch
    assert seq % MOBA_Q_TILE == 0 and MOBA_Q_TILE % MOBA_CHUNK == 0 and MOBA_CHUNK % MOBA_BLOCK == 0
    q_tiles = seq // MOBA_Q_TILE
    seq_blocks = seq // MOBA_BLOCK
    return pl.pallas_call(
        _moba_prompt_kernel,
        grid=(batch, N_HEADS, q_tiles),
        in_specs=[
            pl.BlockSpec((AUG_WIDTH, MOBA_Q_TILE), lambda b, h, j: (b * N_HEADS + h, j)),
            pl.BlockSpec((seq, AUG_WIDTH), lambda b, h, j: (b, h)),
            pl.BlockSpec((seq_blocks, HEAD_DIM, MOBA_BLOCK), lambda b, h, j: (b, h, 0)),
        ],
        out_specs=pl.BlockSpec((MOBA_Q_TILE, HEAD_DIM), lambda b, h, j: (b * q_tiles + j, h)),
        out_shape=jax.ShapeDtypeStruct((rows, D_MODEL), BF16),
        compiler_params=_compiler_params(3),
        name="moba_prompt",
    )(qaug_t, kaug, v_t)


def _moba_sample_kernel(seq, past_len, n_pages, pt_ref, q_ref, kn_ref, vn_ref, slope_ref, *refs):
    del pt_ref
    k_pages = refs[:n_pages]
    v_pages = refs[n_pages:2 * n_pages]
    o_ref = refs[2 * n_pages]
    n_past = n_pages // 2
    rows = seq * N_HEADS
    cols = PAGE_SIZE * N_HEADS
    scale = HEAD_DIM ** -0.5

    q = q_ref[...]
    qb = q.astype(BF16)
    zq = jnp.zeros_like(qb)
    q_pair = jnp.concatenate([jnp.concatenate([qb, zq], axis=1),
                              jnp.concatenate([zq, qb], axis=1)], axis=0)

    def both(x):
        return jnp.concatenate([x, x], axis=0)

    slope = slope_ref[...]
    t_pos = (past_len + lax.broadcasted_iota(jnp.int32, (rows, 1), 0) // N_HEADS).astype(F32)
    r_id = lax.broadcasted_iota(jnp.int32, (2 * rows, cols), 0)
    c_id = lax.broadcasted_iota(jnp.int32, (2 * rows, cols), 1)
    same_head = (r_id % N_HEADS) == (c_id % N_HEADS)
    key_pos = ((r_id // rows) * PAGE_SIZE + c_id // N_HEADS).astype(F32)
    bias0 = jnp.where(same_head, -both(slope) * (both(t_pos) - key_pos), -jnp.inf)

    gates, scores = [], []
    for n in range(n_past):
        k0 = k_pages[2 * n][...]
        k1 = k_pages[2 * n + 1][...]
        k_mean = (jnp.sum(k0, axis=0) + jnp.sum(k1, axis=0)) * (1.0 / MOBA_BLOCK)
        gates.append(jnp.sum(q * jnp.concatenate([k_mean] * seq, axis=0), axis=1, keepdims=True))
        k_pair = jnp.concatenate([k0.reshape(cols, HEAD_DIM).astype(BF16),
                                  k1.reshape(cols, HEAD_DIM).astype(BF16)], axis=1)
        scores.append(_dot_nt(q_pair, k_pair) * scale + (bias0 + both(slope) * float(n * MOBA_BLOCK)))

    selected = []
    for n in range(n_past):
        rank = jnp.zeros((rows, 1), jnp.int32)
        for mth in range(n_past):
            if mth == n:
                continue
            ahead = (gates[mth] >= gates[n]) if mth < n else (gates[mth] > gates[n])
            rank = rank + ahead.astype(jnp.int32)
        selected.append(rank < MOBA_TOPK)

    r_own = lax.broadcasted_iota(jnp.int32, (rows, rows), 0)
    c_own = lax.broadcasted_iota(jnp.int32, (rows, rows), 1)
    own_ok = ((r_own % N_HEADS) == (c_own % N_HEADS)) & (c_own // N_HEADS <= r_own // N_HEADS)
    own_bias = -slope * (t_pos - (past_len + c_own // N_HEADS).astype(F32))
    s_own = jnp.where(own_ok, _dot_nt(qb, kn_ref[...].astype(BF16)) * scale + own_bias, -jnp.inf)

    m = jnp.max(s_own, axis=1, keepdims=True)
    for n in range(n_past):
        bm = jnp.max(scores[n], axis=1, keepdims=True)
        m = jnp.maximum(m, jnp.where(selected[n], jnp.maximum(bm[:rows], bm[rows:]), -jnp.inf))

    p_own = jnp.exp(s_own - m)
    l = jnp.sum(p_own, axis=1, keepdims=True)
    acc = _dot(p_own.astype(BF16), vn_ref[...].astype(BF16))
    for n in range(n_past):
        p = jnp.where(both(selected[n]), jnp.exp(scores[n] - both(m)), 0.0)
        ps = jnp.sum(p, axis=1, keepdims=True)
        l = l + ps[:rows] + ps[rows:]
        v_pair = jnp.concatenate([v_pages[2 * n][...].reshape(cols, HEAD_DIM).astype(BF16),
                                  v_pages[2 * n + 1][...].reshape(cols, HEAD_DIM).astype(BF16)], axis=1)
        pv = _dot(p.astype(BF16), v_pair)
        acc = acc + pv[:rows, :HEAD_DIM] + pv[rows:, HEAD_DIM:]
    o_ref[...] = acc / l


def _moba_sample(page_table, q3, k_new3, v_new3, cache_k, cache_v):
    n_seq, rows, _ = q3.shape
    seq = rows // N_HEADS
    n_pages = page_table.shape[1]
    past_len = n_pages * PAGE_SIZE
    assert MOBA_BLOCK == 2 * PAGE_SIZE and past_len % MOBA_BLOCK == 0 and seq <= MOBA_BLOCK
    assert past_len // MOBA_BLOCK >= MOBA_TOPK
    slopes = jnp.exp2(-ALIBI_MAX_EXP * jnp.arange(1, N_HEADS + 1, dtype=F32) / N_HEADS)
    slope_col = jnp.tile(slopes, seq).reshape(rows, 1)
    seq_spec = pl.BlockSpec((None, rows, HEAD_DIM), lambda b, pt: (b, 0, 0))

    def page_spec(p):
        return pl.BlockSpec((None, PAGE_SIZE, N_HEADS, HEAD_DIM), lambda b, pt: (pt[b, p], 0, 0, 0))

    grid_spec = pltpu.PrefetchScalarGridSpec(
        num_scalar_prefetch=1,
        grid=(n_seq,),
        in_specs=([seq_spec, seq_spec, seq_spec, pl.BlockSpec((rows, 1), lambda b, pt: (0, 0))]
                  + [page_spec(p) for p in range(n_pages)]
                  + [page_spec(p) for p in range(n_pages)]),
        out_specs=seq_spec,
    )
    return pl.pallas_call(
        functools.partial(_moba_sample_kernel, seq, past_len, n_pages),
        grid_spec=grid_spec,
        out_shape=jax.ShapeDtypeStruct((n_seq, rows, HEAD_DIM), F32),
        compiler_params=_compiler_params(1),
        name="moba_sample",
    )(page_table, q3, k_new3, v_new3, slope_col,
      *([cache_k] * n_pages), *([cache_v] * n_pages))


def _bf16_floor(v):
    bits = lax.bitcast_convert_type(v, jnp.uint32) & jnp.uint32(0xFFFF0000)
    return lax.bitcast_convert_type(bits, F32)


def _split3(v):
    hi = _bf16_floor(v)
    r1 = v - hi
    mid = _bf16_floor(r1)
    return [hi, mid, r1 - mid]


def _alibi_rows():
    assert ALIBI_COL0 + 12 <= ALIBI_COL0 + ALIBI_ROWS <= HEAD_DIM
    slopes = jnp.exp2(-ALIBI_MAX_EXP * jnp.arange(1, N_HEADS + 1, dtype=F32) / N_HEADS)
    pieces = jnp.stack(_split3(slopes) * 2, axis=-1)
    k_rows = jnp.zeros((N_HEADS, HEAD_DIM), F32).at[:, ALIBI_COL0 + 6:ALIBI_COL0 + 12].set(pieces)
    q_rows = jnp.zeros((N_HEADS, ALIBI_ROWS), F32).at[:, 0:6].set(pieces)
    return k_rows, jnp.broadcast_to(q_rows[:, :, None], (N_HEADS, ALIBI_ROWS, LANES))


def _gate_means(kmean, batch):
    blocks = kmean.shape[0] // batch
    assert blocks <= N_BLOCK_COLS <= GATE_ROWS
    km = kmean.reshape(batch, blocks, N_HEADS, HEAD_DIM).transpose(0, 2, 1, 3)
    km = jnp.pad(km, ((0, 0), (0, 0), (0, GATE_ROWS - blocks), (0, 0)))
    hi = _bf16_floor(km)
    lo = km - hi
    return jnp.concatenate([hi, hi, lo], axis=-1).astype(BF16)


def kernel(x_prompt, x_sample, state_conv, cache_k, cache_v, cache_mem_k, cache_mem_v, page_table,
           mem_prompt, g_mix, w_in_a, conv_w, w_out_a, g_kv, w_kv, w_in_b, w_out_b,
           g_mem, w_mem_kv, g_mlp, w_up, w_down, g_final):
    batch, seq, _ = x_prompt.shape
    n_seq, dec_seq, _ = x_sample.shape
    assert seq % TOKEN_TILE == 0 and (n_seq * dec_seq) % TOKEN_TILE == 0
    assert TOKEN_TILE % MOBA_BLOCK == 0 and n_seq % MEM_BATCH_TILE == 0

    w_in_a_b, w_out_a_b, w_kv_b = w_in_a.astype(BF16), w_out_a.astype(BF16), w_kv.astype(BF16)
    w_in_b_b, w_out_b_b = w_in_b.astype(BF16), w_out_b.astype(BF16)
    w_mem_kv_b, w_up_b, w_down_b = w_mem_kv.astype(BF16), w_up.astype(BF16), w_down.astype(BF16)

    mem_k, mem_v, mem_kb, mem_vb = _mem_kv(mem_prompt.reshape(batch * N_MEM, D_MODEL), g_mem, w_mem_kv_b)
    k_rows, q_rows = _alibi_rows()

    xp = x_prompt.reshape(batch * seq, D_MODEL)
    xs = x_sample.reshape(n_seq * dec_seq, D_MODEL)
    conv_p, conv_s = [], []
    for layer in range(DEPTH):
        if layer < N_A_LAYERS:
            tok_p, qm_p, ulast = _a_front_prompt(xp, batch, g_mix[layer], w_in_a_b, layer, conv_w[layer])
            conv_p.append(ulast.reshape(batch, -1, SUBLANES, D_MODEL)[:, -1, SUBLANES - (CONV_WIDTH - 1):])
            tok_s, qm_s, u_s = _a_front_sample(xs, dec_seq, g_mix[layer], w_in_a_b, layer, conv_w[layer],
                                               state_conv[layer])
            conv_s.append(u_s.reshape(n_seq, dec_seq, D_MODEL)[:, dec_seq - (CONV_WIDTH - 1):])
            w_out, w_out_layer = w_out_a_b, layer
        else:
            bi = layer - N_A_LAYERS
            qaug_t, qm_p = _b_front_prompt(xp, batch, g_mix[layer], w_in_b_b[bi, :, :D_MODEL].T,
                                           w_in_b_b[bi, :, D_MODEL:], km3, q_rows)
            tok_p = _moba_prompt(qaug_t, kaug, v_t, batch)
            q_s, qm_s = _b_front_sample(xs, g_mix[layer], w_in_b_b, bi)
            tok_s = _moba_sample(page_table, q_s.reshape(n_seq, dec_seq * N_HEADS, HEAD_DIM), k_s3, v_s3,
                                 cache_k, cache_v).reshape(n_seq * dec_seq, D_MODEL)
            w_out, w_out_layer = w_out_b_b, bi
        mo_p = _mem_attn_prompt(qm_p, batch, mem_kb[layer], mem_vb[layer])
        mo_s = _mem_attn_sample(qm_s.reshape(n_seq, dec_seq * MEM_HEADS, MEM_HEAD_DIM), layer, cache_mem_k,
                                cache_mem_v).reshape(n_seq * dec_seq, D_MODEL)
        last = layer == DEPTH - 1
        xp = _post(xp, tok_p, mo_p, w_out, w_out_layer, g_mlp[layer], w_up_b, w_down_b, layer, g_final, last)
        xs = _post(xs, tok_s, mo_s, w_out, w_out_layer, g_mlp[layer], w_up_b, w_down_b, layer, g_final, last)
        if layer == N_A_LAYERS - 1:
            k_p, v_p, kaug, v_t, kmean = _kv_prompt(xp, batch, g_kv, w_kv_b, k_rows)
            km3 = _gate_means(kmean, batch)
            k_s, v_s = _kv_sample(xs, g_kv, w_kv_b)
            k_s3 = k_s.reshape(n_seq, dec_seq * N_HEADS, HEAD_DIM)
            v_s3 = v_s.reshape(n_seq, dec_seq * N_HEADS, HEAD_DIM)

    mem_shape = (DEPTH, batch, N_MEM, MEM_HEADS, MEM_HEAD_DIM)
    return (xp.reshape(batch, seq, D_MODEL),
            xs.reshape(n_seq, dec_seq, D_MODEL),
            jnp.stack(conv_p),
            jnp.stack(conv_s),
            k_p.reshape(batch, seq, N_HEADS, HEAD_DIM),
            v_p.reshape(batch, seq, N_HEADS, HEAD_DIM),
            k_s.reshape(n_seq, dec_seq, N_HEADS, HEAD_DIM),
            v_s.reshape(n_seq, dec_seq, N_HEADS, HEAD_DIM),
            mem_k.reshape(mem_shape),
            mem_v.reshape(mem_shape))
```

```python
import functools

import jax
import jax.numpy as jnp
from jax import lax
from jax.experimental import pallas as pl
from jax.experimental.pallas import tpu as pltpu

F32 = jnp.float32
BF16 = jnp.bfloat16

D_MODEL = 1024
DEPTH = 4
N_A_LAYERS = DEPTH // 2
CONV_WIDTH = 3
N_HEADS = 8
HEAD_DIM = D_MODEL // N_HEADS
MOBA_BLOCK = 256
MOBA_TOPK = 3
N_MEM = 256
MEM_HEADS = 4
MEM_HEAD_DIM = D_MODEL // MEM_HEADS
D_FF = 4 * D_MODEL
ALIBI_MAX_EXP = 8.0
EPS = 1e-5
PAGE_SIZE = 128

LANES = 128
SUBLANES = 8
VMEM_LIMIT_BYTES = 56 * 1024 * 1024

TOKEN_TILE = 512
AUG_WIDTH = 2 * HEAD_DIM
MOBA_CHUNK = 2 * MOBA_BLOCK
MOBA_Q_TILE = 2 * MOBA_CHUNK
ONES_ROWS = 16
N_BLOCK_COLS = 40
GATE_ROWS = 48
ALIBI_COL0 = N_BLOCK_COLS
ALIBI_ROWS = 16
POS_RADIX = 128
MASK_NEG = -(2.0 ** 100)
MEM_BATCH_TILE = 4


def _compiler_params(n_grid_dims):
    return pltpu.CompilerParams(
        dimension_semantics=("arbitrary",) * n_grid_dims,
        vmem_limit_bytes=VMEM_LIMIT_BYTES)


def _resident(shape, layer=None):
    zeros = (0,) * len(shape)
    if layer is None:
        return pl.BlockSpec(shape, lambda *_: zeros, pipeline_mode=pl.Buffered(1))
    return pl.BlockSpec((None,) + tuple(shape), lambda *_: (layer,) + zeros, pipeline_mode=pl.Buffered(1))


def _rms(x, g):
    return x * lax.rsqrt(jnp.mean(x * x, axis=-1, keepdims=True) + EPS) * g


def _dot(a, b):
    return jnp.dot(a, b, preferred_element_type=F32)


def _dot_nt(a, b):
    return lax.dot_general(a, b, (((1,), (1,)), ((), ())), preferred_element_type=F32)


def _reduce_rows(x, op):
    rows, cols = x.shape
    groups = 8
    partial = op(x.reshape(groups, rows // groups, cols), axis=0)
    return op(partial, axis=0, keepdims=True)


def _softmax_pv(s, v):
    m = jnp.max(s, axis=-1, keepdims=True)
    p = jnp.exp(s - m)
    l = jnp.sum(p, axis=-1, keepdims=True)
    return _dot(p.astype(BF16), v) / l


def _mem_kv_kernel(mem_ref, g_ref, w_ref, k_ref, v_ref, kb_ref, vb_ref):
    hn = _rms(mem_ref[...], g_ref[...]).astype(BF16)
    kv = _dot(hn, w_ref[...])
    k = kv[:, :D_MODEL]
    v = kv[:, D_MODEL:]
    k_ref[...] = k
    v_ref[...] = v
    kb_ref[...] = k.astype(BF16)
    vb_ref[...] = v.astype(BF16)


def _mem_kv(mem2d, g_mem, w_mem_kv_b):
    rows = mem2d.shape[0]
    out_f = jax.ShapeDtypeStruct((DEPTH, rows, D_MODEL), F32)
    out_b = jax.ShapeDtypeStruct((DEPTH, rows, D_MODEL), BF16)
    blk = pl.BlockSpec((None, rows, D_MODEL), lambda l: (l, 0, 0))
    return pl.pallas_call(
        _mem_kv_kernel,
        grid=(DEPTH,),
        in_specs=[
            pl.BlockSpec((rows, D_MODEL), lambda l: (0, 0)),
            pl.BlockSpec((None, 1, D_MODEL), lambda l: (l, 0, 0)),
            pl.BlockSpec((None, D_MODEL, 2 * D_MODEL), lambda l: (l, 0, 0)),
        ],
        out_specs=[blk, blk, blk, blk],
        out_shape=[out_f, out_f, out_b, out_b],
        compiler_params=_compiler_params(1),
        name="mem_kv",
    )(mem2d, g_mem.reshape(DEPTH, 1, D_MODEL), w_mem_kv_b)


def _conv_gate(h, w_ref, cw_ref, fix_prev):
    cg = _dot(h, w_ref[:, D_MODEL:2 * D_MODEL])
    hv = _dot(h, w_ref[:, 2 * D_MODEL:3 * D_MODEL])
    u = cg * hv
    prev1, prev2 = fix_prev(pltpu.roll(u, 1, 0), pltpu.roll(u, 2, 0))
    cw = cw_ref[...]
    z = cw[0:1, :] * prev2 + cw[1:2, :] * prev1 + cw[2:3, :] * u
    bg = _dot(h, w_ref[:, 0:D_MODEL])
    qm = _dot(h, w_ref[:, 3 * D_MODEL:4 * D_MODEL])
    return bg * z, qm, u


def _a_front_prompt_kernel(x_ref, g_ref, w_ref, cw_ref, tok_ref, qm_ref, ulast_ref, carry_ref):
    @pl.when(pl.program_id(1) == 0)
    def _():
        carry_ref[...] = jnp.zeros_like(carry_ref)

    h = _rms(x_ref[...], g_ref[...]).astype(BF16)
    carry = carry_ref[...]

    def fix_prev(prev1, prev2):
        row = lax.broadcasted_iota(jnp.int32, prev1.shape, 0)
        c_m1 = carry[SUBLANES - 1:SUBLANES, :]
        c_m2 = carry[SUBLANES - 2:SUBLANES - 1, :]
        prev1 = jnp.where(row < 1, c_m1, prev1)
        prev2 = jnp.where(row < 1, c_m2, jnp.where(row < 2, c_m1, prev2))
        return prev1, prev2

    tok, qm, u = _conv_gate(h, w_ref, cw_ref, fix_prev)
    tok_ref[...] = tok.astype(BF16)
    qm_ref[...] = qm.astype(BF16)
    tail = u[TOKEN_TILE - SUBLANES:, :]
    carry_ref[...] = tail
    ulast_ref[...] = tail


def _a_front_prompt(x2d, batch, g, w_b, layer, conv_w):
    rows = x2d.shape[0]
    tiles = rows // batch // TOKEN_TILE
    tile_spec = pl.BlockSpec((TOKEN_TILE, D_MODEL), lambda b, i: (b * tiles + i, 0))
    return pl.pallas_call(
        _a_front_prompt_kernel,
        grid=(batch, tiles),
        in_specs=[
            tile_spec,
            _resident((1, D_MODEL)),
            _resident((D_MODEL, 4 * D_MODEL), layer),
            _resident((CONV_WIDTH, D_MODEL)),
        ],
        out_specs=[
            tile_spec,
            tile_spec,
            pl.BlockSpec((SUBLANES, D_MODEL), lambda b, i: (b * tiles + i, 0)),
        ],
        out_shape=[
            jax.ShapeDtypeStruct((rows, D_MODEL), BF16),
            jax.ShapeDtypeStruct((rows, D_MODEL), BF16),
            jax.ShapeDtypeStruct((batch * tiles * SUBLANES, D_MODEL), F32),
        ],
        scratch_shapes=[pltpu.VMEM((SUBLANES, D_MODEL), F32)],
        compiler_params=_compiler_params(2),
        name="a_front_prompt",
    )(x2d, g.reshape(1, D_MODEL), w_b, conv_w)


def _a_front_sample_kernel(seq, x_ref, g_ref, w_ref, cw_ref, s1_ref, s2_ref, tok_ref, qm_ref, u_ref):
    h = _rms(x_ref[...], g_ref[...]).astype(BF16)

    def fix_prev(prev1, prev2):
        pos = lax.broadcasted_iota(jnp.int32, prev1.shape, 0) % seq
        return (jnp.where(pos >= 1, prev1, s1_ref[...]),
                jnp.where(pos >= 2, prev2, s2_ref[...]))

    tok, qm, u = _conv_gate(h, w_ref, cw_ref, fix_prev)
    tok_ref[...] = tok.astype(BF16)
    qm_ref[...] = qm
    u_ref[...] = u


def _a_front_sample(x2d, seq, g, w_b, layer, conv_w, state):
    rows = x2d.shape[0]
    n_seq = rows // seq
    assert seq >= CONV_WIDTH - 1
    zeros = jnp.zeros((n_seq, seq - 1, D_MODEL), F32)
    s1 = jnp.concatenate([state[:, 1:2], zeros], axis=1).reshape(rows, D_MODEL)
    s2 = jnp.concatenate([state, zeros[:, 1:]], axis=1).reshape(rows, D_MODEL)
    full = pl.BlockSpec((rows, D_MODEL), lambda i: (0, 0))
    return pl.pallas_call(
        functools.partial(_a_front_sample_kernel, seq),
        grid=(1,),
        in_specs=[full, _resident((1, D_MODEL)), _resident((D_MODEL, 4 * D_MODEL), layer),
                  _resident((CONV_WIDTH, D_MODEL)), full, full],
        out_specs=[full, full, full],
        out_shape=[
            jax.ShapeDtypeStruct((rows, D_MODEL), BF16),
            jax.ShapeDtypeStruct((rows, D_MODEL), F32),
            jax.ShapeDtypeStruct((rows, D_MODEL), F32),
        ],
        compiler_params=_compiler_params(1),
        name="a_front_sample",
    )(x2d, g.reshape(1, D_MODEL), w_b, conv_w, s1, s2)


def _mem_attn_prompt_kernel(q_ref, k_ref, v_ref, o_ref):
    scale = MEM_HEAD_DIM ** -0.5
    for hd in range(MEM_HEADS):
        sl = slice(hd * MEM_HEAD_DIM, (hd + 1) * MEM_HEAD_DIM)
        s = _dot_nt(q_ref[:, sl], k_ref[:, sl]) * scale
        o_ref[:, sl] = _softmax_pv(s, v_ref[:, sl]).astype(o_ref.dtype)


def _mem_attn_prompt(qm, batch, k_b, v_b):
    rows = qm.shape[0]
    tiles = rows // batch // TOKEN_TILE
    tile_spec = pl.BlockSpec((TOKEN_TILE, D_MODEL), lambda b, i: (b * tiles + i, 0))
    kv_spec = pl.BlockSpec((N_MEM, D_MODEL), lambda b, i: (b, 0))
    return pl.pallas_call(
        _mem_attn_prompt_kernel,
        grid=(batch, tiles),
        in_specs=[tile_spec, kv_spec, kv_spec],
        out_specs=tile_spec,
        out_shape=jax.ShapeDtypeStruct((rows, D_MODEL), BF16),
        compiler_params=_compiler_params(2),
        name="mem_attn_prompt",
    )(qm, k_b, v_b)


def _mem_attn_sample_kernel(q_ref, k_ref, v_ref, o_ref):
    scale = MEM_HEAD_DIM ** -0.5
    rows = q_ref.shape[1]
    cols = N_MEM * MEM_HEADS
    same_head = ((lax.broadcasted_iota(jnp.int32, (rows, cols), 0) % MEM_HEADS)
                 == (lax.broadcasted_iota(jnp.int32, (rows, cols), 1) % MEM_HEADS))
    for bb in range(MEM_BATCH_TILE):
        k = k_ref[bb].reshape(cols, MEM_HEAD_DIM).astype(BF16)
        v = v_ref[bb].reshape(cols, MEM_HEAD_DIM).astype(BF16)
        s = jnp.where(same_head, _dot_nt(q_ref[bb].astype(BF16), k) * scale, -jnp.inf)
        o_ref[bb] = _softmax_pv(s, v)


def _mem_attn_sample(qm3, layer, cache_k, cache_v):
    n_seq, rows, _ = qm3.shape
    q_spec = pl.BlockSpec((MEM_BATCH_TILE, rows, MEM_HEAD_DIM), lambda i: (i, 0, 0))
    kv_spec = pl.BlockSpec((None, MEM_BATCH_TILE, N_MEM, MEM_HEADS, MEM_HEAD_DIM),
                           lambda i: (layer, i, 0, 0, 0))
    return pl.pallas_call(
        _mem_attn_sample_kernel,
        grid=(n_seq // MEM_BATCH_TILE,),
        in_specs=[q_spec, kv_spec, kv_spec],
        out_specs=q_spec,
        out_shape=jax.ShapeDtypeStruct((n_seq, rows, MEM_HEAD_DIM), F32),
        compiler_params=_compiler_params(1),
        name="mem_attn_sample",
    )(qm3, cache_k, cache_v)


def _post_kernel(final_norm, x_ref, tok_ref, mo_ref, wo_ref, g_ref, wu_ref, wd_ref, gf_ref, o_ref):
    o_ref[...] = (x_ref[...]
                  + _dot(tok_ref[...].astype(BF16), wo_ref[0:D_MODEL, :])
                  + _dot(mo_ref[...].astype(BF16), wo_ref[D_MODEL:2 * D_MODEL, :]))
    x = o_ref[...]
    h = _rms(x, g_ref[...]).astype(BF16)
    for c in range(D_FF // D_MODEL):
        sl = slice(c * D_MODEL, (c + 1) * D_MODEL)
        a = jnp.square(jnp.maximum(_dot(h, wu_ref[:, sl]), 0.0)).astype(BF16)
        x = x + _dot(a, wd_ref[sl, :])
    o_ref[...] = x
    if final_norm:
        o_ref[...] = _rms(o_ref[...], gf_ref[...])


def _post(x2d, tok, mo, wo_b, wo_layer, g, wu_b, wd_b, layer, g_final, final_norm):
    rows = x2d.shape[0]
    tile_spec = pl.BlockSpec((TOKEN_TILE, D_MODEL), lambda i: (i, 0))
    return pl.pallas_call(
        functools.partial(_post_kernel, final_norm),
        grid=(rows // TOKEN_TILE,),
        in_specs=[
            tile_spec, tile_spec, tile_spec,
            _resident((2 * D_MODEL, D_MODEL), wo_layer),
            _resident((1, D_MODEL)),
            _resident((D_MODEL, D_FF), layer),
            _resident((D_FF, D_MODEL), layer),
            _resident((1, D_MODEL)),
        ],
        out_specs=tile_spec,
        out_shape=jax.ShapeDtypeStruct((rows, D_MODEL), F32),
        compiler_params=_compiler_params(1),
        name="post_final" if final_norm else "post",
    )(x2d, tok, mo, wo_b, g.reshape(1, D_MODEL), wu_b, wd_b, g_final.reshape(1, D_MODEL))


def _pos_digits(pos):
    low = pos & (POS_RADIX - 1)
    return (pos - low).astype(F32), low.astype(F32)


def _kv_prompt_kernel(seq_blocks, x_ref, g_ref, w_ref, krow_ref, k_ref, v_ref, kaug_ref, vt_ref, kmean_ref):
    h = _rms(x_ref[...], g_ref[...]).astype(BF16)
    kv = _dot(h, w_ref[...])
    k = kv[:, :D_MODEL]
    v = kv[:, D_MODEL:]
    k_ref[...] = k
    v_ref[...] = v
    vt_ref[...] = v.T.astype(BF16)
    kmean_ref[...] = jnp.sum(k, axis=0, keepdims=True) * (1.0 / MOBA_BLOCK)
    blk = pl.program_id(0) % seq_blocks
    col = lax.broadcasted_iota(jnp.int32, (MOBA_BLOCK, HEAD_DIM), 1)
    pos_hi, pos_lo = _pos_digits(blk * MOBA_BLOCK + lax.broadcasted_iota(jnp.int32, (MOBA_BLOCK, HEAD_DIM), 0))
    shared = jnp.where(col == blk, 1.0, 0.0)
    shared = jnp.where((col >= ALIBI_COL0) & (col < ALIBI_COL0 + 3), pos_hi, shared)
    shared = jnp.where((col >= ALIBI_COL0 + 3) & (col < ALIBI_COL0 + 6), pos_lo, shared)
    for hd in range(N_HEADS):
        src = slice(hd * HEAD_DIM, (hd + 1) * HEAD_DIM)
        kaug_ref[:, hd * AUG_WIDTH:hd * AUG_WIDTH + HEAD_DIM] = k[:, src].astype(BF16)
        kaug_ref[:, hd * AUG_WIDTH + HEAD_DIM:(hd + 1) * AUG_WIDTH] = (
            shared + krow_ref[hd:hd + 1, :]).astype(BF16)


def _kv_prompt(x2d, batch, g, w_b, k_rows):
    rows = x2d.shape[0]
    n_blocks = rows // MOBA_BLOCK
    seq_blocks = n_blocks // batch
    assert seq_blocks <= N_BLOCK_COLS
    tile = lambda width: pl.BlockSpec((MOBA_BLOCK, width), lambda i: (i, 0))
    return pl.pallas_call(
        functools.partial(_kv_prompt_kernel, seq_blocks),
        grid=(n_blocks,),
        in_specs=[
            tile(D_MODEL),
            _resident((1, D_MODEL)),
            _resident((D_MODEL, 2 * D_MODEL)),
            _resident((N_HEADS, HEAD_DIM)),
        ],
        out_specs=[
            tile(D_MODEL), tile(D_MODEL), tile(N_HEADS * AUG_WIDTH),
            pl.BlockSpec((None, D_MODEL, MOBA_BLOCK), lambda i: (i, 0, 0)),
            pl.BlockSpec((None, 1, D_MODEL), lambda i: (i, 0, 0)),
        ],
        out_shape=[
            jax.ShapeDtypeStruct((rows, D_MODEL), F32),
            jax.ShapeDtypeStruct((rows, D_MODEL), F32),
            jax.ShapeDtypeStruct((rows, N_HEADS * AUG_WIDTH), BF16),
            jax.ShapeDtypeStruct((n_blocks, D_MODEL, MOBA_BLOCK), BF16),
            jax.ShapeDtypeStruct((n_blocks, 1, D_MODEL), F32),
        ],
        compiler_params=_compiler_params(1),
        name="kv_prompt",
    )(x2d, g.reshape(1, D_MODEL), w_b, k_rows)


def _kv_sample_kernel(x_ref, g_ref, w_ref, k_ref, v_ref):
    h = _rms(x_ref[...], g_ref[...]).astype(BF16)
    kv = _dot(h, w_ref[...])
    k_ref[...] = kv[:, :D_MODEL]
    v_ref[...] = kv[:, D_MODEL:]


def _kv_sample(x2d, g, w_b):
    rows = x2d.shape[0]
    tile_spec = pl.BlockSpec((TOKEN_TILE, D_MODEL), lambda i: (i, 0))
    out = jax.ShapeDtypeStruct((rows, D_MODEL), F32)
    return pl.pallas_call(
        _kv_sample_kernel,
        grid=(rows // TOKEN_TILE,),
        in_specs=[tile_spec, _resident((1, D_MODEL)), _resident((D_MODEL, 2 * D_MODEL))],
        out_specs=[tile_spec, tile_spec],
        out_shape=[out, out],
        compiler_params=_compiler_params(1),
        name="kv_sample",
    )(x2d, g.reshape(1, D_MODEL), w_b)


def _b_front_prompt_kernel(x_ref, g_ref, wqt_ref, wm_ref, km_ref, qrow_ref, qaug_ref, qm_ref):
    h = _rms(x_ref[...], g_ref[...]).astype(BF16)
    qm_ref[...] = _dot(h, wm_ref[...]).astype(BF16)
    qt = _dot_nt(wqt_ref[...], h)
    scale = HEAD_DIM ** -0.5
    t0 = pl.program_id(1) * TOKEN_TILE
    shape = (GATE_ROWS, TOKEN_TILE)
    a_shape = (ALIBI_ROWS, TOKEN_TILE)
    a_row = lax.broadcasted_iota(jnp.int32, a_shape, 0)
    t_hi, t_lo = _pos_digits(t0 + lax.broadcasted_iota(jnp.int32, a_shape, 1))
    t_rows = jnp.where((a_row >= 6) & (a_row < 9), -t_hi, jnp.where((a_row >= 9) & (a_row < 12), -t_lo, 0.0))
    tail_rows = jnp.zeros((HEAD_DIM - ALIBI_COL0 - ALIBI_ROWS, TOKEN_TILE), F32)
    blk = lax.broadcasted_iota(jnp.int32, shape, 0)
    own = (t0 + lax.broadcasted_iota(jnp.int32, shape, 1)) // MOBA_BLOCK
    eligible = blk < own
    for hd in range(N_HEADS):
        qh = qt[hd * HEAD_DIM:(hd + 1) * HEAD_DIM, :]
        q_hi = qh.astype(BF16)
        q_lo = (qh - q_hi.astype(F32)).astype(BF16)
        gate = _dot(km_ref[hd], jnp.concatenate([q_hi, q_lo, q_hi], axis=0))
        gate = jnp.where(eligible, gate, -jnp.inf)
        masked = eligible
        for _ in range(MOBA_TOPK):
            best = jnp.max(gate, axis=0, keepdims=True)
            is_best = (gate == best) & (best > -jnp.inf)
            first = jnp.min(jnp.where(is_best, blk, GATE_ROWS), axis=0, keepdims=True)
            pick = blk == first
            masked = masked & jnp.logical_not(pick)
            gate = jnp.where(pick, -jnp.inf, gate)
        mask_rows = jnp.where(masked, MASK_NEG, 0.0)[:ALIBI_COL0]
        alibi_rows = t_rows + jnp.concatenate([qrow_ref[hd]] * (TOKEN_TILE // LANES), axis=1)
        aux = jnp.concatenate([mask_rows, alibi_rows, tail_rows], axis=0)
        qaug_ref[hd * AUG_WIDTH:hd * AUG_WIDTH + HEAD_DIM, :] = (qh * scale).astype(BF16)
        qaug_ref[hd * AUG_WIDTH + HEAD_DIM:(hd + 1) * AUG_WIDTH, :] = aux.astype(BF16)


def _b_front_prompt(x2d, batch, g, wq_t, wm, km3, q_rows):
    rows = x2d.shape[0]
    seq = rows // batch
    tiles = seq // TOKEN_TILE
    tile_spec = pl.BlockSpec((TOKEN_TILE, D_MODEL), lambda b, i: (b * tiles + i, 0))
    return pl.pallas_call(
        _b_front_prompt_kernel,
        grid=(batch, tiles),
        in_specs=[
            tile_spec,
            _resident((1, D_MODEL)),
            _resident((D_MODEL, D_MODEL)),
            _resident((D_MODEL, D_MODEL)),
            pl.BlockSpec((None, N_HEADS, GATE_ROWS, 3 * HEAD_DIM), lambda b, i: (b, 0, 0, 0)),
            _resident((N_HEADS, ALIBI_ROWS, LANES)),
        ],
        out_specs=[pl.BlockSpec((N_HEADS * AUG_WIDTH, TOKEN_TILE), lambda b, i: (b, i)), tile_spec],
        out_shape=[
            jax.ShapeDtypeStruct((batch * N_HEADS * AUG_WIDTH, seq), BF16),
            jax.ShapeDtypeStruct((rows, D_MODEL), BF16),
        ],
        compiler_params=_compiler_params(2),
        name="b_front_prompt",
    )(x2d, g.reshape(1, D_MODEL), wq_t, wm, km3, q_rows)


def _b_front_sample_kernel(x_ref, g_ref, w_ref, q_ref, qm_ref):
    h = _rms(x_ref[...], g_ref[...]).astype(BF16)
    q_ref[...] = _dot(h, w_ref[:, :D_MODEL])
    qm_ref[...] = _dot(h, w_ref[:, D_MODEL:])


def _b_front_sample(x2d, g, w_b, layer):
    rows = x2d.shape[0]
    tile_spec = pl.BlockSpec((TOKEN_TILE, D_MODEL), lambda i: (i, 0))
    return pl.pallas_call(
        _b_front_sample_kernel,
        grid=(rows // TOKEN_TILE,),
        in_specs=[tile_spec, _resident((1, D_MODEL)), _resident((D_MODEL, 2 * D_MODEL), layer)],
        out_specs=[tile_spec, tile_spec],
        out_shape=[jax.ShapeDtypeStruct((rows, D_MODEL), F32),
                   jax.ShapeDtypeStruct((rows, D_MODEL), F32)],
        compiler_params=_compiler_params(1),
        name="b_front_sample",
    )(x2d, g.reshape(1, D_MODEL), w_b)


def _moba_prompt_kernel(q_ref, k_ref, v_ref, o_ref):
    j = pl.program_id(2)
    qt = q_ref[...]
    blocks_per_chunk = MOBA_CHUNK // MOBA_BLOCK
    own_chunks = MOBA_Q_TILE // MOBA_CHUNK
    assert own_chunks % 2 == 0

    def scores(c, causal=False, q0=0):
        start = pl.multiple_of(c * MOBA_CHUNK, MOBA_CHUNK)
        s = _dot(k_ref[pl.ds(start, MOBA_CHUNK), :], qt[:, q0:])
        if causal:
            key = start + lax.broadcasted_iota(jnp.int32, s.shape, 0)
            query = j * MOBA_Q_TILE + q0 + lax.broadcasted_iota(jnp.int32, s.shape, 1)
            s = jnp.where(key <= query, s, MASK_NEG)
        return s

    ones_rows = jnp.ones((ONES_ROWS, MOBA_CHUNK), BF16)

    def values_t(c):
        parts = [v_ref[c * blocks_per_chunk + i] for i in range(blocks_per_chunk)]
        return jnp.concatenate([jnp.concatenate(parts, axis=1), ones_rows], axis=0)

    def probs(s, m):
        return jnp.exp(s - m).astype(BF16)

    first = j * own_chunks
    s = scores(first, causal=True)
    m = _reduce_rows(s, jnp.max)
    state_a = (m, _dot(values_t(first), probs(s, m)))
    for i in range(1, own_chunks):
        q0 = i * MOBA_CHUNK
        m, acc = state_a
        s = scores(first + i, causal=True, q0=q0)
        m_new = jnp.maximum(m[:, q0:], _reduce_rows(s, jnp.max))
        acc_new = jnp.exp(m[:, q0:] - m_new) * acc[:, q0:] + _dot(values_t(first + i), probs(s, m_new))
        state_a = (jnp.concatenate([m[:, :q0], m_new], axis=1),
                   jnp.concatenate([acc[:, :q0], acc_new], axis=1))
    state_b = (jnp.full_like(m, MASK_NEG), jnp.zeros_like(state_a[1]))

    def pair(t, carry):
        (ma, acca), (mb, accb) = carry
        sa = scores(2 * t)
        ma_new = jnp.maximum(ma, _reduce_rows(sa, jnp.max))
        sb = scores(2 * t + 1)
        pa = probs(sa, ma_new)
        mb_new = jnp.maximum(mb, _reduce_rows(sb, jnp.max))
        acca = jnp.exp(ma - ma_new) * acca + _dot(values_t(2 * t), pa)
        pb = probs(sb, mb_new)
        accb = jnp.exp(mb - mb_new) * accb + _dot(values_t(2 * t + 1), pb)
        return (ma_new, acca), (mb_new, accb)

    state_a, state_b = lax.fori_loop(0, j * (own_chunks // 2), pair, (state_a, state_b))

    m = jnp.maximum(state_a[0], state_b[0])
    acc = jnp.exp(state_a[0] - m) * state_a[1] + jnp.exp(state_b[0] - m) * state_b[1]
    out_t = acc[:HEAD_DIM] / acc[HEAD_DIM:HEAD_DIM + 1]
    o_ref[...] = out_t.T.astype(o_ref.dtype)


def _moba_prompt(qaug_t, kaug, v_t, batch):
    rows = kaug.shape[0]
    seq = rows // batch
    assert seq % MOBA_Q_TILE == 0 and MOBA_Q_TILE % MOBA_CHUNK == 0 and MOBA_CHUNK % MOBA_BLOCK == 0
    q_tiles = seq // MOBA_Q_TILE
    seq_blocks = seq // MOBA_BLOCK
    return pl.pallas_call(
        _moba_prompt_kernel,
        grid=(batch, N_HEADS, q_tiles),
        in_specs=[
            pl.BlockSpec((AUG_WIDTH, MOBA_Q_TILE), lambda b, h, j: (b * N_HEADS + h, j)),
            pl.BlockSpec((seq, AUG_WIDTH), lambda b, h, j: (b, h)),
            pl.BlockSpec((seq_blocks, HEAD_DIM, MOBA_BLOCK), lambda b, h, j: (b, h, 0)),
        ],
        out_specs=pl.BlockSpec((MOBA_Q_TILE, HEAD_DIM), lambda b, h, j: (b * q_tiles + j, h)),
        out_shape=jax.ShapeDtypeStruct((rows, D_MODEL), BF16),
        compiler_params=_compiler_params(3),
        name="moba_prompt",
    )(qaug_t, kaug, v_t)


def _moba_sample_kernel(seq, past_len, n_pages, pt_ref, q_ref, kn_ref, vn_ref, slope_ref, *refs):
    del pt_ref
    k_pages = refs[:n_pages]
    v_pages = refs[n_pages:2 * n_pages]
    o_ref = refs[2 * n_pages]
    n_past = n_pages // 2
    rows = seq * N_HEADS
    cols = PAGE_SIZE * N_HEADS
    scale = HEAD_DIM ** -0.5

    q = q_ref[...]
    qb = q.astype(BF16)
    zq = jnp.zeros_like(qb)
    q_pair = jnp.concatenate([jnp.concatenate([qb, zq], axis=1),
                              jnp.concatenate([zq, qb], axis=1)], axis=0)

    def both(x):
        return jnp.concatenate([x, x], axis=0)

    slope = slope_ref[...]
    t_pos = (past_len + lax.broadcasted_iota(jnp.int32, (rows, 1), 0) // N_HEADS).astype(F32)
    r_id = lax.broadcasted_iota(jnp.int32, (2 * rows, cols), 0)
    c_id = lax.broadcasted_iota(jnp.int32, (2 * rows, cols), 1)
    same_head = (r_id % N_HEADS) == (c_id % N_HEADS)
    key_pos = ((r_id // rows) * PAGE_SIZE + c_id // N_HEADS).astype(F32)
    bias0 = jnp.where(same_head, -both(slope) * (both(t_pos) - key_pos), -jnp.inf)

    gates, scores = [], []
    for n in range(n_past):
        k0 = k_pages[2 * n][...]
        k1 = k_pages[2 * n + 1][...]
        k_mean = (jnp.sum(k0, axis=0) + jnp.sum(k1, axis=0)) * (1.0 / MOBA_BLOCK)
        gates.append(jnp.sum(q * jnp.concatenate([k_mean] * seq, axis=0), axis=1, keepdims=True))
        k_pair = jnp.concatenate([k0.reshape(cols, HEAD_DIM).astype(BF16),
                                  k1.reshape(cols, HEAD_DIM).astype(BF16)], axis=1)
        scores.append(_dot_nt(q_pair, k_pair) * scale + (bias0 + both(slope) * float(n * MOBA_BLOCK)))

    selected = []
    for n in range(n_past):
        rank = jnp.zeros((rows, 1), jnp.int32)
        for mth in range(n_past):
            if mth == n:
                continue
            ahead = (gates[mth] >= gates[n]) if mth < n else (gates[mth] > gates[n])
            rank = rank + ahead.astype(jnp.int32)
        selected.append(rank < MOBA_TOPK)

    r_own = lax.broadcasted_iota(jnp.int32, (rows, rows), 0)
    c_own = lax.broadcasted_iota(jnp.int32, (rows, rows), 1)
    own_ok = ((r_own % N_HEADS) == (c_own % N_HEADS)) & (c_own // N_HEADS <= r_own // N_HEADS)
    own_bias = -slope * (t_pos - (past_len + c_own // N_HEADS).astype(F32))
    s_own = jnp.where(own_ok, _dot_nt(qb, kn_ref[...].astype(BF16)) * scale + own_bias, -jnp.inf)

    m = jnp.max(s_own, axis=1, keepdims=True)
    for n in range(n_past):
        bm = jnp.max(scores[n], axis=1, keepdims=True)
        m = jnp.maximum(m, jnp.where(selected[n], jnp.maximum(bm[:rows], bm[rows:]), -jnp.inf))

    p_own = jnp.exp(s_own - m)
    l = jnp.sum(p_own, axis=1, keepdims=True)
    acc = _dot(p_own.astype(BF16), vn_ref[...].astype(BF16))
    for n in range(n_past):
        p = jnp.where(both(selected[n]), jnp.exp(scores[n] - both(m)), 0.0)
        ps = jnp.sum(p, axis=1, keepdims=True)
        l = l + ps[:rows] + ps[rows:]
        v_pair = jnp.concatenate([v_pages[2 * n][...].reshape(cols, HEAD_DIM).astype(BF16),
                                  v_pages[2 * n + 1][...].reshape(cols, HEAD_DIM).astype(BF16)], axis=1)
        pv = _dot(p.astype(BF16), v_pair)
        acc = acc + pv[:rows, :HEAD_DIM] + pv[rows:, HEAD_DIM:]
    o_ref[...] = acc / l


def _moba_sample(page_table, q3, k_new3, v_new3, cache_k, cache_v):
    n_seq, rows, _ = q3.shape
    seq = rows // N_HEADS
    n_pages = page_table.shape[1]
    past_len = n_pages * PAGE_SIZE
    assert MOBA_BLOCK == 2 * PAGE_SIZE and past_len % MOBA_BLOCK == 0 and seq <= MOBA_BLOCK
    assert past_len // MOBA_BLOCK >= MOBA_TOPK
    slopes = jnp.exp2(-ALIBI_MAX_EXP * jnp.arange(1, N_HEADS + 1, dtype=F32) / N_HEADS)
    slope_col = jnp.tile(slopes, seq).reshape(rows, 1)
    seq_spec = pl.BlockSpec((None, rows, HEAD_DIM), lambda b, pt: (b, 0, 0))

    def page_spec(p):
        return pl.BlockSpec((None, PAGE_SIZE, N_HEADS, HEAD_DIM), lambda b, pt: (pt[b, p], 0, 0, 0))

    grid_spec = pltpu.PrefetchScalarGridSpec(
        num_scalar_prefetch=1,
        grid=(n_seq,),
        in_specs=([seq_spec, seq_spec, seq_spec, pl.BlockSpec((rows, 1), lambda b, pt: (0, 0))]
                  + [page_spec(p) for p in range(n_pages)]
                  + [page_spec(p) for p in range(n_pages)]),
        out_specs=seq_spec,
    )
    return pl.pallas_call(
        functools.partial(_moba_sample_kernel, seq, past_len, n_pages),
        grid_spec=grid_spec,
        out_shape=jax.ShapeDtypeStruct((n_seq, rows, HEAD_DIM), F32),
        compiler_params=_compiler_params(1),
        name="moba_sample",
    )(page_table, q3, k_new3, v_new3, slope_col,
      *([cache_k] * n_pages), *([cache_v] * n_pages))


def _bf16_floor(v):
    bits = lax.bitcast_convert_type(v, jnp.uint32) & jnp.uint32(0xFFFF0000)
    return lax.bitcast_convert_type(bits, F32)


def _split3(v):
    hi = _bf16_floor(v)
    r1 = v - hi
    mid = _bf16_floor(r1)
    return [hi, mid, r1 - mid]


def _alibi_rows():
    assert ALIBI_COL0 + 12 <= ALIBI_COL0 + ALIBI_ROWS <= HEAD_DIM
    slopes = jnp.exp2(-ALIBI_MAX_EXP * jnp.arange(1, N_HEADS + 1, dtype=F32) / N_HEADS)
    pieces = jnp.stack(_split3(slopes) * 2, axis=-1)
    k_rows = jnp.zeros((N_HEADS, HEAD_DIM), F32).at[:, ALIBI_COL0 + 6:ALIBI_COL0 + 12].set(pieces)
    q_rows = jnp.zeros((N_HEADS, ALIBI_ROWS), F32).at[:, 0:6].set(pieces)
    return k_rows, jnp.broadcast_to(q_rows[:, :, None], (N_HEADS, ALIBI_ROWS, LANES))


def _gate_means(kmean, batch):
    blocks = kmean.shape[0] // batch
    assert blocks <= N_BLOCK_COLS <= GATE_ROWS
    km = kmean.reshape(batch, blocks, N_HEADS, HEAD_DIM).transpose(0, 2, 1, 3)
    km = jnp.pad(km, ((0, 0), (0, 0), (0, GATE_ROWS - blocks), (0, 0)))
    hi = _bf16_floor(km)
    lo = km - hi
    return jnp.concatenate([hi, hi, lo], axis=-1).astype(BF16)


def kernel(x_prompt, x_sample, state_conv, cache_k, cache_v, cache_mem_k, cache_mem_v, page_table,
           mem_prompt, g_mix, w_in_a, conv_w, w_out_a, g_kv, w_kv, w_in_b, w_out_b,
           g_mem, w_mem_kv, g_mlp, w_up, w_down, g_final):
    batch, seq, _ = x_prompt.shape
    n_seq, dec_seq, _ = x_sample.shape
    assert seq % TOKEN_TILE == 0 and (n_seq * dec_seq) % TOKEN_TILE == 0
    assert TOKEN_TILE % MOBA_BLOCK == 0 and n_seq % MEM_BATCH_TILE == 0

    w_in_a_b, w_out_a_b, w_kv_b = w_in_a.astype(BF16), w_out_a.astype(BF16), w_kv.astype(BF16)
    w_in_b_b, w_out_b_b = w_in_b.astype(BF16), w_out_b.astype(BF16)
    w_mem_kv_b, w_up_b, w_down_b = w_mem_kv.astype(BF16), w_up.astype(BF16), w_down.astype(BF16)

    mem_k, mem_v, mem_kb, mem_vb = _mem_kv(mem_prompt.reshape(batch * N_MEM, D_MODEL), g_mem, w_mem_kv_b)
    k_rows, q_rows = _alibi_rows()

    xp = x_prompt.reshape(batch * seq, D_MODEL)
    xs = x_sample.reshape(n_seq * dec_seq, D_MODEL)
    conv_p, conv_s = [], []
    for layer in range(DEPTH):
        if layer < N_A_LAYERS:
            tok_p, qm_p, ulast = _a_front_prompt(xp, batch, g_mix[layer], w_in_a_b, layer, conv_w[layer])
            conv_p.append(ulast.reshape(batch, -1, SUBLANES, D_MODEL)[:, -1, SUBLANES - (CONV_WIDTH - 1):])
            tok_s, qm_s, u_s = _a_front_sample(xs, dec_seq, g_mix[layer], w_in_a_b, layer, conv_w[layer],
                                               state_conv[layer])
            conv_s.append(u_s.reshape(n_seq, dec_seq, D_MODEL)[:, dec_seq - (CONV_WIDTH - 1):])
            w_out, w_out_layer = w_out_a_b, layer
        else:
            bi = layer - N_A_LAYERS
            qaug_t, qm_p = _b_front_prompt(xp, batch, g_mix[layer], w_in_b_b[bi, :, :D_MODEL].T,
                                           w_in_b_b[bi, :, D_MODEL:], km3, q_rows)
            tok_p = _moba_prompt(qaug_t, kaug, v_t, batch)
            q_s, qm_s = _b_front_sample(xs, g_mix[layer], w_in_b_b, bi)
            tok_s = _moba_sample(page_table, q_s.reshape(n_seq, dec_seq * N_HEADS, HEAD_DIM), k_s3, v_s3,
                                 cache_k, cache_v).reshape(n_seq * dec_seq, D_MODEL)
            w_out, w_out_layer = w_out_b_b, bi
        mo_p = _mem_attn_prompt(qm_p, batch, mem_kb[layer], mem_vb[layer])
        mo_s = _mem_attn_sample(qm_s.reshape(n_seq, dec_seq * MEM_HEADS, MEM_HEAD_DIM), layer, cache_mem_k,
                                cache_mem_v).reshape(n_seq * dec_seq, D_MODEL)
        last = layer == DEPTH - 1
        xp = _post(xp, tok_p, mo_p, w_out, w_out_layer, g_mlp[layer], w_up_b, w_down_b, layer, g_final, last)
        xs = _post(xs, tok_s, mo_s, w_out, w_out_layer, g_mlp[layer], w_up_b, w_down_b, layer, g_final, last)
        if layer == N_A_LAYERS - 1:
            k_p, v_p, kaug, v_t, kmean = _kv_prompt(xp, batch, g_kv, w_kv_b, k_rows)
            km3 = _gate_means(kmean, batch)
            k_s, v_s = _kv_sample(xs, g_kv, w_kv_b)
            k_s3 = k_s.reshape(n_seq, dec_seq * N_HEADS, HEAD_DIM)
            v_s3 = v_s.reshape(n_seq, dec_seq * N_HEADS, HEAD_DIM)

    mem_shape = (DEPTH, batch, N_MEM, MEM_HEADS, MEM_HEAD_DIM)
    return (xp.reshape(batch, seq, D_MODEL),
            xs.reshape(n_seq, dec_seq, D_MODEL),
            jnp.stack(conv_p),
            jnp.stack(conv_s),
            k_p.reshape(batch, seq, N_HEADS, HEAD_DIM),
            v_p.reshape(batch, seq, N_HEADS, HEAD_DIM),
            k_s.reshape(n_seq, dec_seq, N_HEADS, HEAD_DIM),
            v_s.reshape(n_seq, dec_seq, N_HEADS, HEAD_DIM),
            mem_k.reshape(mem_shape),
            mem_v.reshape(mem_shape))
```
